```python
import math
import jax, jax.numpy as jnp
from jax import lax
import numpy as np

D_MODEL = 2048
BATCH = 16
SEQ = 2048
DEPTH = 2

D_MIX = D_MODEL
SB_HEADS = 8
SB_HEAD_DIM = 128
SB_WIDTH = SB_HEADS * SB_HEAD_DIM
Q_BLOCK = 128
SSD_WIDTH = D_MIX - SB_WIDTH
SSD_HEAD_DIM = 64
SSD_HEADS = SSD_WIDTH // SSD_HEAD_DIM
SSD_GROUPS = 2
SSD_HEADS_PER_GROUP = SSD_HEADS // SSD_GROUPS
SSD_STATE = 128
SSD_CONV = 4
SSD_CHUNK = 128
SSD_CONV_CH = SSD_WIDTH + 2 * SSD_GROUPS * SSD_STATE
IN_PROJ = 3 * SB_WIDTH + SSD_WIDTH + SSD_CONV_CH + SSD_HEADS
N_EXPERTS = 64
TOP_K = 8
N_EXPERT_GROUPS = 8
TOPK_GROUPS = 4
D_EXPERT = 512
D_SHARED = 512
ROUTED_SCALE = 2.5
MOE_BLOCK = 128
DEEPNORM_ALPHA = (2 * DEPTH) ** 0.25
DEEPNORM_BETA = (8 * DEPTH) ** -0.25
LN_EPS = 1e-5
RMS_EPS = 1e-6
ADA_GAIN = 0.1

kernel_name = "stickbreak_ssd_moe_deepnorm_adaln"


def layer_norm_plain(x):
    xf = x.astype(jnp.float32)
    mu = jnp.mean(xf, -1, keepdims=True)
    var = jnp.mean(jnp.square(xf - mu), -1, keepdims=True)
    return ((xf - mu) * lax.rsqrt(var + LN_EPS)).astype(x.dtype)


def layer_norm(x, g, b):
    xf = x.astype(jnp.float32)
    mu = jnp.mean(xf, -1, keepdims=True)
    var = jnp.mean(jnp.square(xf - mu), -1, keepdims=True)
    y = (xf - mu) * lax.rsqrt(var + LN_EPS) * g.astype(jnp.float32) + b.astype(jnp.float32)
    return y.astype(x.dtype)


def rms_norm(x, g):
    xf = x.astype(jnp.float32)
    y = xf * lax.rsqrt(jnp.mean(jnp.square(xf), -1, keepdims=True) + RMS_EPS)
    return y * g.astype(jnp.float32)


def stick_breaking_attention(q, k, v):
    b, s, h, d = q.shape
    nq = s // Q_BLOCK
    scale = 1.0 / math.sqrt(d)
    kpos = jnp.arange(s)
    q_blocks = jnp.moveaxis(q.reshape(b, nq, Q_BLOCK, h, d), 1, 0)

    def block(args):
        q_blk, i = args
        tpos = i * Q_BLOCK + jnp.arange(Q_BLOCK)
        z = jnp.einsum('bqhd,bkhd->bhqk', q_blk, k,
                       preferred_element_type=jnp.float32) * scale
        mask = kpos[None, :] < tpos[:, None]
        log_one_minus = jnp.where(mask, -jax.nn.softplus(z), 0.0)
        between = lax.cumsum(log_one_minus, axis=3, reverse=True) - log_one_minus
        w = jnp.where(mask, jnp.exp(jax.nn.log_sigmoid(z) + between), 0.0)
        return jnp.einsum('bhqk,bkhd->bqhd', w.astype(v.dtype), v)

    o = lax.map(block, (q_blocks, jnp.arange(nq)))
    return jnp.moveaxis(o, 0, 1).reshape(b, s, h, d)


def causal_depthwise_conv(x, w, bias):
    ch = x.shape[-1]
    y = lax.conv_general_dilated(x, w.astype(x.dtype)[:, None, :], (1,), [(SSD_CONV - 1, 0)],
                                 dimension_numbers=('NWC', 'WIO', 'NWC'),
                                 feature_group_count=ch)
    return y + bias.astype(x.dtype)


def segsum(a):
    l = a.shape[-1]
    rep = jnp.broadcast_to(a[..., :, None], a.shape + (l,))
    strict = jnp.tril(jnp.ones((l, l), bool), -1)
    cs = jnp.cumsum(jnp.where(strict, rep, 0.0), axis=-2)
    return jnp.where(jnp.tril(jnp.ones((l, l), bool)), cs, -jnp.inf)


def ssd_chunked(xs, dt, A, Bm, Cm):
    b, s, g, e, p = xs.shape
    n = Bm.shape[-1]
    c = s // SSD_CHUNK
    xdt = (xs.astype(jnp.float32) * dt[..., None]).reshape(b, c, SSD_CHUNK, g, e, p)
    a = (dt * A).reshape(b, c, SSD_CHUNK, g, e).transpose(0, 1, 3, 4, 2)
    Bc = Bm.astype(jnp.float32).reshape(b, c, SSD_CHUNK, g, n)
    Cc = Cm.astype(jnp.float32).reshape(b, c, SSD_CHUNK, g, n)
    a_cum = jnp.cumsum(a, axis=-1)
    Lmat = jnp.exp(segsum(a))
    y_diag = jnp.einsum('bclgn,bcsgn,bcgels,bcsgep->bclgep', Cc, Bc, Lmat, xdt)
    decay_states = jnp.exp(a_cum[..., -1:] - a_cum)
    states = jnp.einsum('bclgn,bcgel,bclgep->bcgepn', Bc, decay_states, xdt)
    chunk_decay = jnp.exp(a_cum[..., -1])

    def step(state, inp):
        st, dec = inp
        return state * dec[..., None, None] + st, state

    init = jnp.zeros((b, g, e, p, n), jnp.float32)
    _, prev = lax.scan(step, init, (jnp.moveaxis(states, 1, 0), jnp.moveaxis(chunk_decay, 1, 0)))
    prev = jnp.moveaxis(prev, 0, 1)
    y_off = jnp.einsum('bclgn,bcgepn,bcgel->bclgep', Cc, prev, jnp.exp(a_cum))
    return (y_diag + y_off).reshape(b, s, g, e, p)


def hybrid_mixer(h, w_in, conv_w, conv_b, dt_bias, a_log, d_skip, sb_norm_w, ssd_norm_w, w_out):
    b, s, _ = h.shape
    proj = h @ w_in
    cuts = [SB_WIDTH, 2 * SB_WIDTH, 3 * SB_WIDTH, 3 * SB_WIDTH + SSD_WIDTH,
            3 * SB_WIDTH + SSD_WIDTH + SSD_CONV_CH]
    q, k, v, z, xbc, dt = jnp.split(proj, cuts, axis=-1)
    q = q.reshape(b, s, SB_HEADS, SB_HEAD_DIM)
    k = k.reshape(b, s, SB_HEADS, SB_HEAD_DIM)
    v = v.reshape(b, s, SB_HEADS, SB_HEAD_DIM)
    o_sb = stick_breaking_attention(q, k, v)
    o_sb = rms_norm(o_sb, sb_norm_w.reshape(SB_HEADS, SB_HEAD_DIM)).reshape(b, s, SB_WIDTH)
    xbc = jax.nn.silu(causal_depthwise_conv(xbc, conv_w, conv_b))
    xs, Bm, Cm = jnp.split(xbc, [SSD_WIDTH, SSD_WIDTH + SSD_GROUPS * SSD_STATE], axis=-1)
    xs = xs.reshape(b, s, SSD_GROUPS, SSD_HEADS_PER_GROUP, SSD_HEAD_DIM)
    Bm = Bm.reshape(b, s, SSD_GROUPS, SSD_STATE)
    Cm = Cm.reshape(b, s, SSD_GROUPS, SSD_STATE)
    dt = jax.nn.softplus(dt.astype(jnp.float32) + dt_bias.astype(jnp.float32))
    dt = dt.reshape(b, s, SSD_GROUPS, SSD_HEADS_PER_GROUP)
    A = -jnp.exp(a_log.astype(jnp.float32)).reshape(SSD_GROUPS, SSD_HEADS_PER_GROUP)
    y = ssd_chunked(xs, dt, A, Bm, Cm)
    y = y + xs.astype(jnp.float32) * d_skip.astype(jnp.float32).reshape(SSD_GROUPS, SSD_HEADS_PER_GROUP, 1)
    y = y.reshape(b, s, SSD_WIDTH) * jax.nn.silu(z.astype(jnp.float32))
    o_ssd = rms_norm(y, ssd_norm_w)
    mixed = jnp.concatenate([o_sb, o_ssd], axis=-1).astype(h.dtype)
    return mixed @ w_out


def moe_ffn(h, w_router, router_bias, w_gate, w_up, w_down, ws_gate, ws_up, ws_down):
    b, s, d = h.shape
    t = b * s
    hf = h.reshape(t, d)
    scores = jax.nn.sigmoid(jnp.dot(hf, w_router, preferred_element_type=jnp.float32))
    biased = scores + router_bias.astype(jnp.float32)
    grouped = biased.reshape(t, N_EXPERT_GROUPS, N_EXPERTS // N_EXPERT_GROUPS)
    group_score = jnp.sum(lax.top_k(grouped, 2)[0], axis=-1)
    _, top_groups = lax.top_k(group_score, TOPK_GROUPS)
    group_mask = jnp.any(top_groups[..., None] == jnp.arange(N_EXPERT_GROUPS), axis=1)
    expert_mask = jnp.repeat(group_mask, N_EXPERTS // N_EXPERT_GROUPS, axis=1)
    _, idx = lax.top_k(jnp.where(expert_mask, biased, -jnp.inf), TOP_K)
    sel = jnp.take_along_axis(scores, idx, axis=1)
    gates = sel / jnp.sum(sel, -1, keepdims=True) * ROUTED_SCALE
    n_assign = t * TOP_K
    e_flat = idx.reshape(-1)
    tok_flat = jnp.repeat(jnp.arange(t, dtype=jnp.int32), TOP_K)
    g_flat = gates.reshape(-1)
    order = jnp.argsort(e_flat)
    e_sorted = e_flat[order]
    counts = jnp.bincount(e_flat, length=N_EXPERTS)
    padded = (counts + MOE_BLOCK - 1) // MOE_BLOCK * MOE_BLOCK
    start = jnp.cumsum(counts) - counts
    pend = jnp.cumsum(padded)
    pstart = pend - padded
    dest = pstart[e_sorted] + jnp.arange(n_assign) - start[e_sorted]
    cap = n_assign + N_EXPERTS * MOE_BLOCK
    n_blocks = cap // MOE_BLOCK
    row_tok = jnp.zeros((cap,), jnp.int32).at[dest].set(tok_flat[order])
    row_gate = jnp.zeros((cap,), jnp.float32).at[dest].set(g_flat[order])
    block_expert = jnp.minimum(
        jnp.searchsorted(pend, jnp.arange(n_blocks) * MOE_BLOCK, side='right'), N_EXPERTS - 1)

    def expert_block(y, blk):
        tok, gate, e = blk
        xb = hf[tok]
        o = (jax.nn.silu(xb @ w_gate[e]) * (xb @ w_up[e])) @ w_down[e]
        return y.at[tok].add(o * gate[:, None].astype(o.dtype)), None

    y, _ = lax.scan(expert_block, jnp.zeros_like(hf),
                    (row_tok.reshape(n_blocks, MOE_BLOCK), row_gate.reshape(n_blocks, MOE_BLOCK),
                     block_expert))
    shared = (jax.nn.silu(hf @ ws_gate) * (hf @ ws_up)) @ ws_down
    return (y + shared).reshape(b, s, d)


def setup_inputs(seed: int = 0) -> dict:
    key = jax.random.key(seed)
    ks = jax.random.split(key, 26)
    f32 = jnp.float32
    L = DEPTH

    def nrm(k, shape, fan_in, gain=1.0):
        return jax.random.normal(k, shape, f32) * (gain * fan_in ** -0.5)

    def noise(k, shape, scale):
        return jax.random.normal(k, shape, f32) * scale

    dt0 = jnp.exp(jax.random.uniform(ks[7], (L, SSD_HEADS), f32, math.log(1e-3), math.log(1e-1)))
    return {
        "x": jax.random.normal(ks[0], (BATCH, SEQ, D_MODEL), f32),
        "c": jax.random.normal(ks[1], (BATCH, D_MODEL), f32),
        "w_ada": nrm(ks[2], (L, D_MODEL, 6 * D_MODEL), D_MODEL, ADA_GAIN),
        "b_ada": noise(ks[3], (L, 6 * D_MODEL), 0.01),
        "w_in": nrm(ks[4], (L, D_MODEL, IN_PROJ), D_MODEL),
        "conv_w": nrm(ks[5], (L, SSD_CONV, SSD_CONV_CH), SSD_CONV),
        "conv_b": noise(ks[6], (L, SSD_CONV_CH), 0.01),
        "dt_bias": dt0 + jnp.log(-jnp.expm1(-dt0)),
        "a_log": jnp.log(jax.random.uniform(ks[8], (L, SSD_HEADS), f32, 1.0, 16.0)),
        "d_skip": 1.0 + noise(ks[9], (L, SSD_HEADS), 0.1),
        "sb_norm_w": 1.0 + noise(ks[10], (L, SB_WIDTH), 0.02),
        "ssd_norm_w": 1.0 + noise(ks[11], (L, SSD_WIDTH), 0.02),
        "w_out": nrm(ks[12], (L, D_MIX, D_MODEL), D_MIX, DEEPNORM_BETA),
        "ln1_g": 1.0 + noise(ks[13], (L, D_MODEL), 0.02),
        "ln1_b": noise(ks[14], (L, D_MODEL), 0.01),
        "w_router": nrm(ks[15], (L, D_MODEL, N_EXPERTS), D_MODEL),
        "router_bias": noise(ks[16], (L, N_EXPERTS), 0.01),
        "w_gate": nrm(ks[17], (L, N_EXPERTS, D_MODEL, D_EXPERT), D_MODEL),
        "w_up": nrm(ks[18], (L, N_EXPERTS, D_MODEL, D_EXPERT), D_MODEL),
        "w_down": nrm(ks[19], (L, N_EXPERTS, D_EXPERT, D_MODEL), D_EXPERT, DEEPNORM_BETA),
        "ws_gate": nrm(ks[20], (L, D_MODEL, D_SHARED), D_MODEL),
        "ws_up": nrm(ks[21], (L, D_MODEL, D_SHARED), D_MODEL),
        "ws_down": nrm(ks[22], (L, D_SHARED, D_MODEL), D_SHARED, DEEPNORM_BETA),
        "ln2_g": 1.0 + noise(ks[23], (L, D_MODEL), 0.02),
        "ln2_b": noise(ks[24], (L, D_MODEL), 0.01),
    }


def reference(x, c, w_ada, b_ada, w_in, conv_w, conv_b, dt_bias, a_log, d_skip, sb_norm_w,
              ssd_norm_w, w_out, ln1_g, ln1_b, w_router, router_bias, w_gate, w_up, w_down,
              ws_gate, ws_up, ws_down, ln2_g, ln2_b):
    c_act = jax.nn.silu(c)
    for l in range(DEPTH):
        mod = c_act @ w_ada[l] + b_ada[l]
        shift1, scale1, gate1, shift2, scale2, gate2 = [m[:, None, :] for m in jnp.split(mod, 6, axis=-1)]
        h = layer_norm_plain(x) * (1.0 + scale1) + shift1
        m = hybrid_mixer(h, w_in[l], conv_w[l], conv_b[l], dt_bias[l], a_log[l], d_skip[l],
                         sb_norm_w[l], ssd_norm_w[l], w_out[l])
        x = layer_norm(DEEPNORM_ALPHA * x + (1.0 + gate1) * m, ln1_g[l], ln1_b[l])
        h = layer_norm_plain(x) * (1.0 + scale2) + shift2
        f = moe_ffn(h, w_router[l], router_bias[l], w_gate[l], w_up[l], w_down[l],
                    ws_gate[l], ws_up[l], ws_down[l])
        x = layer_norm(DEEPNORM_ALPHA * x + (1.0 + gate2) * f, ln2_g[l], ln2_b[l])
    return x
```

```python
import functools
import math

import jax
import jax.numpy as jnp
from jax import lax
from jax.experimental import pallas as pl
from jax.experimental.pallas import tpu as pltpu

F32 = jnp.float32
BF16 = jnp.bfloat16

SB_HEAD_DIM = 128
SSD_HEAD_DIM = 64
SSD_GROUPS = 2
SSD_STATE = 128
SSD_CHUNK = 128
TOP_K = 8
N_EXPERT_GROUPS = 8
TOPK_GROUPS = 4
ROUTED_SCALE = 2.5
LN_EPS = 1e-5
RMS_EPS = 1e-6

VMEM_LIMIT_BYTES = 56 * 1024 * 1024


def _params(sem, vmem=VMEM_LIMIT_BYTES):
    return pltpu.CompilerParams(dimension_semantics=sem, vmem_limit_bytes=vmem)


def _pick(n, cands):
    for c in cands:
        if n % c == 0:
            return c
    raise ValueError(f"no tile in {cands} divides {n}")


def _softplus(z):
    return jnp.maximum(z, 0.0) + jnp.log1p(jnp.exp(-jnp.abs(z)))


def _silu(v):
    return v * jax.nn.sigmoid(v)


def _split3(v):
    hi = v.astype(BF16)
    r1 = v - hi.astype(F32)
    mid = r1.astype(BF16)
    lo = (r1 - mid.astype(F32)).astype(BF16)
    return hi, mid, lo


def _dot(a, b):
    return jnp.dot(a, b, preferred_element_type=F32)


def _dot_nt(a, b):
    return lax.dot_general(a, b, (((1,), (1,)), ((), ())), preferred_element_type=F32)


def _dot_tn(a, b):
    return lax.dot_general(a, b, (((0,), (0,)), ((), ())), preferred_element_type=F32)


def _ln_stats(v):
    mu = jnp.mean(v, axis=-1, keepdims=True)
    d = v - mu
    var = jnp.mean(d * d, axis=-1, keepdims=True)
    return d * lax.rsqrt(var + LN_EPS)


def _ada_kernel(c_ref, w_ref, b_ref, o_ref):
    ca = _silu(c_ref[...]).astype(BF16)
    o_ref[0] = _dot(ca, w_ref[0].astype(BF16)) + b_ref[0]


def _ada_call(c, w_ada, b_ada):
    depth, d, n = w_ada.shape
    b = c.shape[0]
    tn = _pick(n, (512, 256, 128))
    return pl.pallas_call(
        _ada_kernel,
        out_shape=jax.ShapeDtypeStruct((depth, b, n), F32),
        grid=(depth, n // tn),
        in_specs=[
            pl.BlockSpec((b, d), lambda l, j: (0, 0)),
            pl.BlockSpec((1, d, tn), lambda l, j: (l, 0, j)),
            pl.BlockSpec((1, 1, tn), lambda l, j: (l, 0, j)),
        ],
        out_specs=pl.BlockSpec((1, b, tn), lambda l, j: (l, 0, j)),
        compiler_params=_params(("arbitrary", "arbitrary")),
        name="ada_mod",
    )(c, w_ada, b_ada.reshape(depth, 1, n))


def _inproj_kernel(x_ref, mod_ref, w_ref, wdt_ref, o_ref, dt_ref, h_scr):
    @pl.when(pl.program_id(1) == 0)
    def _():
        xn = _ln_stats(x_ref[...])
        h = xn * (1.0 + mod_ref[0, 1:2, :]) + mod_ref[0, 0:1, :]
        hb = h.astype(BF16)
        h_scr[...] = hb
        dt_ref[...] = _dot(hb, wdt_ref[...])

    o_ref[...] = _dot(h_scr[...], w_ref[...]).astype(BF16)


def _inproj_call(x2d, mod, w_main, w_dt, seq):
    t, d = x2d.shape
    n = w_main.shape[1]
    tm = _pick(seq, (512, 256, 128))
    tn = _pick(n, (512, 256, 128))
    per_b = seq // tm
    return pl.pallas_call(
        _inproj_kernel,
        out_shape=(jax.ShapeDtypeStruct((t, n), BF16), jax.ShapeDtypeStruct((t, 128), F32)),
        grid=(t // tm, n // tn),
        in_specs=[
            pl.BlockSpec((tm, d), lambda i, j: (i, 0)),
            pl.BlockSpec((1, 6, d), lambda i, j: (i // per_b, 0, 0)),
            pl.BlockSpec((d, tn), lambda i, j: (0, j)),
            pl.BlockSpec((d, 128), lambda i, j: (0, 0)),
        ],
        out_specs=(
            pl.BlockSpec((tm, tn), lambda i, j: (i, j)),
            pl.BlockSpec((tm, 128), lambda i, j: (i, 0)),
        ),
        scratch_shapes=[pltpu.VMEM((tm, d), BF16)],
        compiler_params=_params(("arbitrary", "arbitrary")),
        name="in_proj",
    )(x2d, mod, w_main, w_dt)


def _sb_kernel(q_ref, k_ref, v_ref, nw_ref, o_ref, *, tq):
    qi = pl.program_id(2)
    q = q_ref[...]
    hd = q.shape[1]
    scale = 1.0 / math.sqrt(hd)
    rj = lax.broadcasted_iota(jnp.int32, (tq, tq), 0)
    cs = lax.broadcasted_iota(jnp.int32, (tq, tq), 1)
    later = jnp.where(rj > cs, 1.0, 0.0).astype(BF16)

    def body(i, carry):
        acc, run = carry
        kb = qi - i
        start = pl.multiple_of(kb * tq, tq)
        k = k_ref[pl.ds(start, tq), :]
        v = v_ref[pl.ds(start, tq), :]
        z = _dot_nt(q, k) * scale
        sp = _softplus(z)
        mask = (cs + kb * tq) < (rj + qi * tq)
        lom = jnp.where(mask, -sp, 0.0)
        hi = lom.astype(BF16)
        lo = (lom - hi.astype(F32)).astype(BF16)
        between = _dot(hi, later) + _dot(lo, later) + run
        w = jnp.where(mask, jnp.exp(z - sp + between), 0.0)
        acc = acc + _dot(w.astype(BF16), v)
        run = run + jnp.sum(lom, axis=1, keepdims=True)
        return acc, run

    acc, _ = lax.fori_loop(0, qi + 1, body, (jnp.zeros((tq, hd), F32), jnp.zeros((tq, 1), F32)))
    o = acc * lax.rsqrt(jnp.mean(acc * acc, axis=-1, keepdims=True) + RMS_EPS) * nw_ref[...]
    o_ref[...] = o.astype(BF16)


def _sb_call(proj, sb_norm_w, batch, seq, heads):
    t = proj.shape[0]
    tq = 128
    nq = seq // tq
    return pl.pallas_call(
        functools.partial(_sb_kernel, tq=tq),
        out_shape=jax.ShapeDtypeStruct((t, heads * SB_HEAD_DIM), BF16),
        grid=(batch, heads, nq),
        in_specs=[
            pl.BlockSpec((tq, SB_HEAD_DIM), lambda b, h, i: (b * nq + i, h)),
            pl.BlockSpec((seq, SB_HEAD_DIM), lambda b, h, i: (b, heads + h)),
            pl.BlockSpec((seq, SB_HEAD_DIM), lambda b, h, i: (b, 2 * heads + h)),
            pl.BlockSpec((1, SB_HEAD_DIM), lambda b, h, i: (0, h)),
        ],
        out_specs=pl.BlockSpec((tq, SB_HEAD_DIM), lambda b, h, i: (b * nq + i, h)),
        compiler_params=_params(("arbitrary", "arbitrary", "arbitrary")),
        name="sb_attn",
    )(proj, proj, proj, sb_norm_w.reshape(1, -1))


def _ssd_kernel(z_ref, xs_ref, b_ref, c_ref, dt_ref, dtt_ref, cw_ref, cb_ref, dtb_ref, dtbt_ref,
                alog_ref, alogt_ref, dskip_ref, nw_ref, o_ref, prev_scr, state_scr, *, heads):
    ln = xs_ref.shape[0]
    w = xs_ref.shape[1]
    gn = b_ref.shape[1]
    n = gn // SSD_GROUPS
    wg = w // SSD_GROUPS
    hpg = heads // SSD_GROUPS

    @pl.when(pl.program_id(1) == 0)
    def _():
        prev_scr[...] = jnp.zeros_like(prev_scr)
        state_scr[...] = jnp.zeros_like(state_scr)

    raw = jnp.concatenate([xs_ref[...], b_ref[...], c_ref[...]], axis=1).astype(F32)
    pr = prev_scr[...]
    ch = raw.shape[1]
    row = lax.broadcasted_iota(jnp.int32, (ln, ch), 0)
    taps = cw_ref.shape[0]
    acc = raw * cw_ref[taps - 1:taps, :] + cb_ref[...]
    for sh in range(1, taps):
        shifted = jnp.where(row < sh, pltpu.roll(pr, sh, 0), pltpu.roll(raw, sh, 0))
        acc = acc + shifted * cw_ref[taps - 1 - sh:taps - sh, :]
    prev_scr[...] = raw
    xbc = _silu(acc)
    xs = xbc[:, :w]
    bm = xbc[:, w:w + gn]
    cm = xbc[:, w + gn:]

    dtv = _softplus(dt_ref[0] + dtb_ref[...])
    dtvt = _softplus(dtt_ref[0] + dtbt_ref[...])
    a_full = -jnp.exp(alog_ref[...])
    a_t = -jnp.exp(alogt_ref[...])

    eh = lax.broadcasted_iota(jnp.int32, (heads, w), 0)
    ec = lax.broadcasted_iota(jnp.int32, (heads, w), 1)
    expand = jnp.where(ec // SSD_HEAD_DIM == eh, 1.0, 0.0).astype(BF16)
    d_hi, d_mid, d_lo = _split3(dtv)
    dt_exp = _dot(d_hi, expand) + _dot(d_mid, expand) + _dot(d_lo, expand)

    rl = lax.broadcasted_iota(jnp.int32, (ln, ln), 0)
    cl = lax.broadcasted_iota(jnp.int32, (ln, ln), 1)
    tril = cl <= rl
    incl = jnp.where(tril, 1.0, 0.0).astype(BF16)
    incl_t = jnp.where(rl <= cl, 1.0, 0.0).astype(BF16)

    a_exp = dt_exp * a_full
    a_hi, a_mid, a_lo = _split3(a_exp)
    acum = _dot(incl, a_hi) + _dot(incl, a_mid) + _dot(incl, a_lo)
    at = dtvt * a_t
    t_hi, t_mid, t_lo = _split3(at)
    acum_t = _dot(t_hi, incl_t) + _dot(t_mid, incl_t) + _dot(t_lo, incl_t)

    xdt = xs * dt_exp
    last = acum[ln - 1:ln, :]
    xdd = (xdt * jnp.exp(last - acum)).astype(BF16)
    xdtb = xdt.astype(BF16)
    ea = jnp.exp(acum)
    cdec = jnp.exp(last)

    col_head = lax.broadcasted_iota(jnp.int32, (ln, wg), 1) // SSD_HEAD_DIM
    ys = []
    for g in range(SSD_GROUPS):
        bg = bm[:, g * n:(g + 1) * n].astype(BF16)
        cg = cm[:, g * n:(g + 1) * n].astype(BF16)
        cb = _dot_nt(cg, bg)
        ms = []
        for e in range(hpg):
            h = g * hpg + e
            ac = acum[:, h * SSD_HEAD_DIM:h * SSD_HEAD_DIM + 1]
            ar = acum_t[h:h + 1, :]
            lmat = jnp.exp(jnp.where(tril, ac - ar, -jnp.inf))
            ms.append((cb * lmat).astype(BF16))
        mcat = jnp.concatenate(ms, axis=1)
        xg = xdtb[:, g * wg:(g + 1) * wg]
        xbd = jnp.concatenate([jnp.where(col_head == e, xg, jnp.zeros_like(xg)) for e in range(hpg)], axis=0)
        y_diag = _dot(mcat, xbd)
        sg = state_scr[g]
        y_off = _dot(cg, sg.astype(BF16)) * ea[:, g * wg:(g + 1) * wg]
        s_new = _dot_tn(bg, xdd[:, g * wg:(g + 1) * wg])
        state_scr[g] = sg * cdec[:, g * wg:(g + 1) * wg] + s_new
        ys.append(y_diag + y_off)
    y = jnp.concatenate(ys, axis=1) + xs * dskip_ref[...]
    y = y * _silu(z_ref[...].astype(F32))
    o = y * lax.rsqrt(jnp.mean(y * y, axis=-1, keepdims=True) + RMS_EPS) * nw_ref[...]
    o_ref[...] = o.astype(BF16)


def _ssd_call(proj, dt, conv_w, conv_b, dt_bias, a_log, d_skip, ssd_norm_w, batch, seq, sbw):
    t = proj.shape[0]
    heads = dt_bias.shape[0]
    w = heads * SSD_HEAD_DIM
    gn = SSD_GROUPS * SSD_STATE
    ch = w + 2 * gn
    ln = SSD_CHUNK
    nc = seq // ln
    dt3 = dt.reshape(batch, seq, heads)
    dtt = jnp.swapaxes(dt3, 1, 2)
    rep = lambda v: jnp.repeat(v, SSD_HEAD_DIM).reshape(1, w)
    z_blk, xs_blk = 3 * sbw // w, (3 * sbw + w) // w
    b_blk, c_blk = (3 * sbw + 2 * w) // gn, (3 * sbw + 2 * w + gn) // gn
    assert 3 * sbw % w == 0 and (3 * sbw + 2 * w) % gn == 0
    full = lambda shape: pl.BlockSpec(shape, lambda b, c: (0,) * len(shape))
    return pl.pallas_call(
        functools.partial(_ssd_kernel, heads=heads),
        out_shape=jax.ShapeDtypeStruct((t, w), BF16),
        grid=(batch, nc),
        in_specs=[
            pl.BlockSpec((ln, w), lambda b, c: (b * nc + c, z_blk)),
            pl.BlockSpec((ln, w), lambda b, c: (b * nc + c, xs_blk)),
            pl.BlockSpec((ln, gn), lambda b, c: (b * nc + c, b_blk)),
            pl.BlockSpec((ln, gn), lambda b, c: (b * nc + c, c_blk)),
            pl.BlockSpec((1, ln, heads), lambda b, c: (b, c, 0)),
            pl.BlockSpec((1, heads, ln), lambda b, c: (b, 0, c)),
            full(conv_w.shape), full((1, ch)), full((1, heads)), full((heads, 1)),
            full((1, w)), full((heads, 1)), full((1, w)), full((1, w)),
        ],
        out_specs=pl.BlockSpec((ln, w), lambda b, c: (b * nc + c, 0)),
        scratch_shapes=[pltpu.VMEM((ln, ch), F32), pltpu.VMEM((SSD_GROUPS, SSD_STATE, w // SSD_GROUPS), F32)],
        compiler_params=_params(("arbitrary", "arbitrary")),
        name="ssd",
    )(proj, proj, proj, proj, dt3, dtt, conv_w, conv_b.reshape(1, ch), dt_bias.reshape(1, heads),
      dt_bias.reshape(heads, 1), rep(a_log), a_log.reshape(heads, 1), rep(d_skip), ssd_norm_w.reshape(1, w))


def _outproj_kernel(osb_ref, ossd_ref, x_ref, mod_ref, w1_ref, w2_ref, g_ref, b_ref, x1_ref, h2_ref, *, alpha):
    m = _dot(osb_ref[...], w1_ref[...]) + _dot(ossd_ref[...], w2_ref[...])
    v = alpha * x_ref[...] + (1.0 + mod_ref[0, 2:3, :]) * m
    x1 = _ln_stats(v) * g_ref[...] + b_ref[...]
    x1_ref[...] = x1
    h2_ref[...] = _ln_stats(x1) * (1.0 + mod_ref[0, 4:5, :]) + mod_ref[0, 3:4, :]


def _outproj_call(o_sb, o_ssd, x2d, mod, w_out_b, ln_g, ln_b, seq, alpha):
    t, d = x2d.shape
    sbw, ssw = o_sb.shape[1], o_ssd.shape[1]
    tm = _pick(seq, (256, 128))
    per_b = seq // tm
    return pl.pallas_call(
        functools.partial(_outproj_kernel, alpha=alpha),
        out_shape=(jax.ShapeDtypeStruct((t, d), F32), jax.ShapeDtypeStruct((t, d), F32)),
        grid=(t // tm,),
        in_specs=[
            pl.BlockSpec((tm, sbw), lambda i: (i, 0)),
            pl.BlockSpec((tm, ssw), lambda i: (i, 0)),
            pl.BlockSpec((tm, d), lambda i: (i, 0)),
            pl.BlockSpec((1, 6, d), lambda i: (i // per_b, 0, 0)),
            pl.BlockSpec((sbw, d), lambda i: (0, 0)),
            pl.BlockSpec((ssw, d), lambda i: (0, 0)),
            pl.BlockSpec((1, d), lambda i: (0, 0)),
            pl.BlockSpec((1, d), lambda i: (0, 0)),
        ],
        out_specs=(pl.BlockSpec((tm, d), lambda i: (i, 0)), pl.BlockSpec((tm, d), lambda i: (i, 0))),
        compiler_params=_params(("arbitrary",)),
        name="out_proj",
    )(o_sb, o_ssd, x2d, mod, w_out_b[:sbw], w_out_b[sbw:], ln_g.reshape(1, d), ln_b.reshape(1, d))


def _router_kernel(h_ref, wr_ref, bias_ref, idx_ref, rnk_ref, gate_ref, cnt_ref, carry_scr):
    ne = wr_ref.shape[0]
    tm = h_ref.shape[0]
    per_g = ne // N_EXPERT_GROUPS
    neg = -jnp.inf

    @pl.when(pl.program_id(0) == 0)
    def _():
        carry_scr[...] = jnp.zeros_like(carry_scr)

    logits = _dot_nt(wr_ref[...], h_ref[...].astype(BF16))
    scores = jax.nn.sigmoid(logits)
    biased = scores + bias_ref[...]
    s3 = scores.reshape(N_EXPERT_GROUPS, per_g, tm)
    b3 = biased.reshape(N_EXPERT_GROUPS, per_g, tm)
    shape3 = (N_EXPERT_GROUPS, per_g, tm)
    j_io = lax.broadcasted_iota(jnp.int32, shape3, 1)
    g_io = lax.broadcasted_iota(jnp.int32, shape3, 0)
    e_io = g_io * per_g + j_io

    m1 = jnp.max(b3, axis=1, keepdims=True)
    i1 = jnp.min(jnp.where(b3 == m1, j_io, per_g), axis=1, keepdims=True)
    m2 = jnp.max(jnp.where(j_io == i1, neg, b3), axis=1, keepdims=True)
    gs = m1 + m2
    gg = lax.broadcasted_iota(jnp.int32, gs.shape, 0)
    gmask = jnp.zeros(gs.shape, F32)
    cur = gs
    for _ in range(TOPK_GROUPS):
        m = jnp.max(cur, axis=0, keepdims=True)
        gi = jnp.min(jnp.where(cur == m, gg, N_EXPERT_GROUPS), axis=0, keepdims=True)
        pick = gg == gi
        gmask = jnp.where(pick, 1.0, gmask)
        cur = jnp.where(pick, neg, cur)
    cur = jnp.where(jnp.broadcast_to(gmask, shape3) > 0.5, b3, neg)

    sel = jnp.zeros(shape3, F32)
    picks = []
    for _ in range(TOP_K):
        m = jnp.max(jnp.max(cur, axis=1, keepdims=True), axis=0, keepdims=True)
        ei = jnp.min(jnp.min(jnp.where(cur == m, e_io, ne), axis=1, keepdims=True), axis=0, keepdims=True)
        pick = e_io == ei
        sel = jnp.where(pick, 1.0, sel)
        cur = jnp.where(pick, neg, cur)
        picks.append((pick, ei))

    selw = sel * s3
    denom = jnp.sum(jnp.sum(selw, axis=1, keepdims=True), axis=0, keepdims=True)
    gates3 = selw / denom * ROUTED_SCALE

    sel2 = sel.reshape(ne, tm)
    rj = lax.broadcasted_iota(jnp.int32, (tm, tm), 0)
    cs = lax.broadcasted_iota(jnp.int32, (tm, tm), 1)
    before = jnp.where(rj < cs, 1.0, 0.0).astype(BF16)
    carry = carry_scr[...]
    rank2 = _dot(sel2.astype(BF16), before) + carry[:, 0:1]
    rank3 = rank2.reshape(shape3)
    new_carry = carry + jnp.sum(sel2, axis=1, keepdims=True)
    carry_scr[...] = new_carry
    cnt_ref[...] = new_carry.astype(jnp.int32)

    for k, (pick, ei) in enumerate(picks):
        red = lambda v: jnp.sum(jnp.sum(v, axis=1, keepdims=True), axis=0, keepdims=True).reshape(1, tm)
        idx_ref[k:k + 1, :] = ei.reshape(1, tm)
        rnk_ref[k:k + 1, :] = red(jnp.where(pick, rank3, 0.0)).astype(jnp.int32)
        gate_ref[k:k + 1, :] = red(jnp.where(pick, gates3, 0.0))


def _router_call(h2, wr_t, bias):
    t, d = h2.shape
    ne = wr_t.shape[0]
    tm = _pick(t, (512, 256))
    if t == tm:
        tm = tm // 2
    return pl.pallas_call(
        _router_kernel,
        out_shape=(jax.ShapeDtypeStruct((TOP_K, t), jnp.int32), jax.ShapeDtypeStruct((TOP_K, t), jnp.int32),
                   jax.ShapeDtypeStruct((TOP_K, t), F32), jax.ShapeDtypeStruct((ne, 128), jnp.int32)),
        grid=(t // tm,),
        in_specs=[
            pl.BlockSpec((tm, d), lambda i: (i, 0)),
            pl.BlockSpec((ne, d), lambda i: (0, 0)),
            pl.BlockSpec((ne, 1), lambda i: (0, 0)),
        ],
        out_specs=(pl.BlockSpec((TOP_K, tm), lambda i: (0, i)), pl.BlockSpec((TOP_K, tm), lambda i: (0, i)),
                   pl.BlockSpec((TOP_K, tm), lambda i: (0, i)), pl.BlockSpec((ne, 128), lambda i: (0, 0))),
        scratch_shapes=[pltpu.VMEM((ne, 128), F32)],
        compiler_params=_params(("arbitrary",)),
        name="router",
    )(h2, wr_t, bias.reshape(ne, 1))


def _expert_kernel(be_ref, nu_ref, tok_ref, h_hbm, wg_ref, wu_ref, wd_ref, o_ref, xbuf, idx_smem, isem, gsem):
    tb = xbuf.shape[0]

    @pl.when(pl.program_id(0) < nu_ref[0])
    def _():
        cp = pltpu.make_async_copy(tok_ref.at[0, 0], idx_smem, isem)
        cp.start()
        cp.wait()

        def issue(i, c):
            pltpu.make_async_copy(h_hbm.at[idx_smem[i]], xbuf.at[i], gsem).start()
            return c

        lax.fori_loop(0, tb, issue, 0, unroll=8)
        pltpu.make_async_copy(h_hbm.at[pl.ds(0, tb)], xbuf, gsem).wait()
        x = xbuf[...].astype(BF16)
        a = (_silu(_dot(x, wg_ref[0])) * _dot(x, wu_ref[0])).astype(BF16)
        o_ref[...] = _dot(a, wd_ref[0])

    @pl.when(pl.program_id(0) >= nu_ref[0])
    def _():
        o_ref[...] = jnp.zeros_like(o_ref)


def _expert_call(block_expert, n_used, row_tok, h2, wg, wu, wd, tb):
    t, d = h2.shape
    ne, _, de = wg.shape
    nblk = row_tok.shape[0] // tb
    blk = lambda r, be, nu: jnp.minimum(r, nu[0] - 1)
    grid_spec = pltpu.PrefetchScalarGridSpec(
        num_scalar_prefetch=2,
        grid=(nblk,),
        in_specs=[
            pl.BlockSpec((1, 1, tb), lambda r, be, nu: (blk(r, be, nu), 0, 0)),
            pl.BlockSpec(memory_space=pl.ANY),
            pl.BlockSpec((1, d, de), lambda r, be, nu: (be[blk(r, be, nu)], 0, 0)),
            pl.BlockSpec((1, d, de), lambda r, be, nu: (be[blk(r, be, nu)], 0, 0)),
            pl.BlockSpec((1, de, d), lambda r, be, nu: (be[blk(r, be, nu)], 0, 0)),
        ],
        out_specs=pl.BlockSpec((tb, d), lambda r, be, nu: (r, 0)),
        scratch_shapes=[pltpu.VMEM((tb, d), F32), pltpu.SMEM((tb,), jnp.int32),
                        pltpu.SemaphoreType.DMA, pltpu.SemaphoreType.DMA],
    )
    return pl.pallas_call(
        _expert_kernel,
        out_shape=jax.ShapeDtypeStruct((nblk * tb, d), F32),
        grid_spec=grid_spec,
        compiler_params=_params(("arbitrary",)),
        name="experts",
    )(block_expert, n_used, row_tok.reshape(nblk, 1, tb), h2, wg, wu, wd)


def _combine_kernel(pos_ref, gate_ref, h_ref, x_ref, mod_ref, o_hbm, sg_ref, su_ref, sd_ref, g_ref, b_ref,
                    out_ref, gbuf, idx_smem, isem, gsem, *, alpha):
    tm = h_ref.shape[0]
    cp = pltpu.make_async_copy(pos_ref.at[0, 0], idx_smem, isem)
    cp.start()
    cp.wait()

    def issue(i, c):
        pltpu.make_async_copy(o_hbm.at[idx_smem[i]], gbuf.at[i], gsem).start()
        return c

    lax.fori_loop(0, TOP_K * tm, issue, 0, unroll=8)
    hb = h_ref[...].astype(BF16)
    a = (_silu(_dot(hb, sg_ref[...])) * _dot(hb, su_ref[...])).astype(BF16)
    f = _dot(a, sd_ref[...])
    pltpu.make_async_copy(o_hbm.at[pl.ds(0, TOP_K * tm)], gbuf, gsem).wait()
    gate = gate_ref[...]
    for k in range(TOP_K):
        f = f + gbuf[pl.ds(k * tm, tm), :] * gate[:, k:k + 1]
    v = alpha * x_ref[...] + (1.0 + mod_ref[0, 5:6, :]) * f
    out_ref[...] = _ln_stats(v) * g_ref[...] + b_ref[...]


def _combine_call(pos, gate_t, h2, x1, mod, o_sorted, sg, su, sd, ln_g, ln_b, seq, alpha):
    t, d = h2.shape
    ds = sg.shape[1]
    tm = 128
    per_b = seq // tm
    nt = t // tm
    return pl.pallas_call(
        functools.partial(_combine_kernel, alpha=alpha),
        out_shape=jax.ShapeDtypeStruct((t, d), F32),
        grid=(nt,),
        in_specs=[
            pl.BlockSpec((1, 1, TOP_K * tm), lambda i: (i, 0, 0)),
            pl.BlockSpec((tm, TOP_K), lambda i: (i, 0)),
            pl.BlockSpec((tm, d), lambda i: (i, 0)),
            pl.BlockSpec((tm, d), lambda i: (i, 0)),
            pl.BlockSpec((1, 6, d), lambda i: (i // per_b, 0, 0)),
            pl.BlockSpec(memory_space=pl.ANY),
            pl.BlockSpec((d, ds), lambda i: (0, 0)),
            pl.BlockSpec((d, ds), lambda i: (0, 0)),
            pl.BlockSpec((ds, d), lambda i: (0, 0)),
            pl.BlockSpec((1, d), lambda i: (0, 0)),
            pl.BlockSpec((1, d), lambda i: (0, 0)),
        ],
        out_specs=pl.BlockSpec((tm, d), lambda i: (i, 0)),
        scratch_shapes=[pltpu.VMEM((TOP_K * tm, d), F32), pltpu.SMEM((TOP_K * tm,), jnp.int32),
                        pltpu.SemaphoreType.DMA, pltpu.SemaphoreType.DMA],
        compiler_params=_params(("arbitrary",)),
        name="combine",
    )(pos, gate_t, h2, x1, mod, o_sorted, sg, su, sd, ln_g.reshape(1, d), ln_b.reshape(1, d))


def _moe(h2, x1, mod, wr_t, router_bias, wg, wu, wd, sg, su, sd, ln_g, ln_b, seq, alpha, tb):
    t, d = h2.shape
    ne = wg.shape[0]
    idx, rnk, gate, cnt = _router_call(h2, wr_t, router_bias)
    counts = cnt[:, 0]
    padded = (counts + tb - 1) // tb * tb
    pend = jnp.cumsum(padded)
    pstart = pend - padded
    dest = pstart[idx] + rnk
    nblk = (t * TOP_K) // tb + ne
    tok = jnp.broadcast_to(jnp.arange(t, dtype=jnp.int32)[None, :], (TOP_K, t))
    row_tok = jnp.zeros((nblk * tb,), jnp.int32).at[dest.reshape(-1)].set(tok.reshape(-1))
    block_expert = jnp.minimum(jnp.searchsorted(pend, jnp.arange(nblk, dtype=jnp.int32) * tb, side="right"),
                               ne - 1).astype(jnp.int32)
    n_used = (pend[-1] // tb).astype(jnp.int32).reshape(1)
    o_sorted = _expert_call(block_expert, n_used, row_tok, h2, wg, wu, wd, tb)
    tm = 128
    pos = dest.reshape(TOP_K, t // tm, tm).transpose(1, 0, 2).reshape(t // tm, 1, TOP_K * tm)
    return _combine_call(pos, gate.T, h2, x1, mod, o_sorted, sg, su, sd, ln_g, ln_b, seq, alpha)


def kernel(x, c, w_ada, b_ada, w_in, conv_w, conv_b, dt_bias, a_log, d_skip, sb_norm_w, ssd_norm_w, w_out, ln1_g,
           ln1_b, w_router, router_bias, w_gate, w_up, w_down, ws_gate, ws_up, ws_down, ln2_g, ln2_b):
    batch, seq, d = x.shape
    depth = w_ada.shape[0]
    sbw = sb_norm_w.shape[1]
    sb_heads = sbw // SB_HEAD_DIM
    ssd_heads = dt_bias.shape[1]
    n_main = w_in.shape[2] - ssd_heads
    alpha = (2 * depth) ** 0.25
    tb = 256 if (batch * seq * TOP_K) // w_gate.shape[1] >= 1024 else 128

    mod_all = _ada_call(c, w_ada, b_ada).reshape(depth, batch, 6, d)
    x2d = x.reshape(batch * seq, d)
    for l in range(depth):
        mod = mod_all[l]
        w_main = w_in[l, :, :n_main].astype(BF16)
        w_dt = jnp.pad(w_in[l, :, n_main:], ((0, 0), (0, 128 - ssd_heads))).astype(BF16)
        proj, dtp = _inproj_call(x2d, mod, w_main, w_dt, seq)
        o_sb = _sb_call(proj, sb_norm_w[l], batch, seq, sb_heads)
        o_ssd = _ssd_call(proj, dtp[:, :ssd_heads], conv_w[l], conv_b[l], dt_bias[l], a_log[l], d_skip[l],
                          ssd_norm_w[l], batch, seq, sbw)
        x1, h2 = _outproj_call(o_sb, o_ssd, x2d, mod, w_out[l].astype(BF16), ln1_g[l], ln1_b[l], seq, alpha)
        x2d = _moe(h2, x1, mod, w_router[l].T.astype(BF16), router_bias[l], w_gate[l].astype(BF16),
                   w_up[l].astype(BF16), w_down[l].astype(BF16), ws_gate[l].astype(BF16), ws_up[l].astype(BF16),
                   ws_down[l].astype(BF16), ln2_g[l], ln2_b[l], seq, alpha, tb)
    return x2d.reshape(batch, seq, d)
```

```python
import functools
import math

import jax
import jax.numpy as jnp
from jax import lax
from jax.experimental import pallas as pl
from jax.experimental.pallas import tpu as pltpu

F32 = jnp.float32
BF16 = jnp.bfloat16

SB_HEAD_DIM = 128
SSD_HEAD_DIM = 64
SSD_GROUPS = 2
SSD_STATE = 128
SSD_CHUNK = 128
TOP_K = 8
N_EXPERT_GROUPS = 8
TOPK_GROUPS = 4
ROUTED_SCALE = 2.5
LN_EPS = 1e-5
RMS_EPS = 1e-6
SB_DEAD_LOG = -110.0

VMEM_LIMIT_BYTES = 56 * 1024 * 1024


def _params(sem, vmem=VMEM_LIMIT_BYTES):
    return pltpu.CompilerParams(dimension_semantics=sem, vmem_limit_bytes=vmem)


def _pick(n, cands):
    for c in cands:
        if n % c == 0:
            return c
    raise ValueError(f"no tile in {cands} divides {n}")


def _softplus(z):
    return jnp.maximum(z, 0.0) + jnp.log(1.0 + jnp.exp(-jnp.abs(z)))


def _silu(v):
    return v * jax.nn.sigmoid(v)


def _split3(v):
    hi = v.astype(BF16)
    r1 = v - hi.astype(F32)
    mid = r1.astype(BF16)
    lo = (r1 - mid.astype(F32)).astype(BF16)
    return hi, mid, lo


def _dot(a, b):
    return jnp.dot(a, b, preferred_element_type=F32)


def _dot_nt(a, b):
    return lax.dot_general(a, b, (((1,), (1,)), ((), ())), preferred_element_type=F32)


def _dot_tn(a, b):
    return lax.dot_general(a, b, (((0,), (0,)), ((), ())), preferred_element_type=F32)


def _ln_stats(v):
    mu = jnp.mean(v, axis=-1, keepdims=True)
    d = v - mu
    var = jnp.mean(d * d, axis=-1, keepdims=True)
    return d * lax.rsqrt(var + LN_EPS)


def _ada_kernel(c_ref, w_ref, b_ref, o_ref):
    ca = _silu(c_ref[...]).astype(BF16)
    o_ref[0] = _dot(ca, w_ref[0].astype(BF16)) + b_ref[0]


def _ada_call(c, w_ada, b_ada):
    depth, d, n = w_ada.shape
    b = c.shape[0]
    tn = _pick(n, (512, 256, 128))
    return pl.pallas_call(
        _ada_kernel,
        out_shape=jax.ShapeDtypeStruct((depth, b, n), F32),
        grid=(depth, n // tn),
        in_specs=[
            pl.BlockSpec((b, d), lambda l, j: (0, 0)),
            pl.BlockSpec((1, d, tn), lambda l, j: (l, 0, j)),
            pl.BlockSpec((1, 1, tn), lambda l, j: (l, 0, j)),
        ],
        out_specs=pl.BlockSpec((1, b, tn), lambda l, j: (l, 0, j)),
        compiler_params=_params(("arbitrary", "arbitrary")),
        name="ada_mod",
    )(c, w_ada, b_ada.reshape(depth, 1, n))


def _inproj_kernel(x_ref, mod_ref, w_ref, wdt_ref, o_ref, dt_ref, h_scr):
    @pl.when(pl.program_id(1) == 0)
    def _():
        xn = _ln_stats(x_ref[...])
        h = xn * (1.0 + mod_ref[0, 1:2, :]) + mod_ref[0, 0:1, :]
        hb = h.astype(BF16)
        h_scr[...] = hb
        dt_ref[...] = _dot(hb, wdt_ref[...])

    o_ref[...] = _dot(h_scr[...], w_ref[...]).astype(BF16)


def _inproj_call(x2d, mod, w_main, w_dt, seq):
    t, d = x2d.shape
    n = w_main.shape[1]
    tm = _pick(seq, (512, 256, 128))
    tn = _pick(n, (512, 256, 128))
    per_b = seq // tm
    return pl.pallas_call(
        _inproj_kernel,
        out_shape=(jax.ShapeDtypeStruct((t, n), BF16), jax.ShapeDtypeStruct((t, 128), F32)),
        grid=(t // tm, n // tn),
        in_specs=[
            pl.BlockSpec((tm, d), lambda i, j: (i, 0)),
            pl.BlockSpec((1, 6, d), lambda i, j: (i // per_b, 0, 0)),
            pl.BlockSpec((d, tn), lambda i, j: (0, j)),
            pl.BlockSpec((d, 128), lambda i, j: (0, 0)),
        ],
        out_specs=(
            pl.BlockSpec((tm, tn), lambda i, j: (i, j)),
            pl.BlockSpec((tm, 128), lambda i, j: (i, 0)),
        ),
        scratch_shapes=[pltpu.VMEM((tm, d), BF16)],
        compiler_params=_params(("arbitrary", "arbitrary")),
        name="in_proj",
    )(x2d, mod, w_main, w_dt)


def _sb_kernel(q_ref, k_ref, v_ref, nw_ref, o_ref, *, tq, tk, hg):
    qi = pl.program_id(2)
    hd = SB_HEAD_DIM
    scale = 1.0 / math.sqrt(hd)
    nd = tq // tk
    rj = lax.broadcasted_iota(jnp.int32, (tq, tk), 0)
    cs = lax.broadcasted_iota(jnp.int32, (tq, tk), 1)
    lr = lax.broadcasted_iota(jnp.int32, (tk, tk), 0)
    lc = lax.broadcasted_iota(jnp.int32, (tk, tk), 1)
    later = jnp.where(lr > lc, 1.0, 0.0).astype(BF16)
    qs = [q_ref[:, g * hd:(g + 1) * hd] for g in range(hg)]

    def block(kb, accs, runs, offset):
        start = pl.multiple_of(kb * tk, tk)
        new_accs, new_runs = [], []
        for g in range(hg):
            k = k_ref[pl.ds(start, tk), g * hd:(g + 1) * hd]
            v = v_ref[pl.ds(start, tk), g * hd:(g + 1) * hd]
            z = _dot_nt(qs[g], k) * scale
            sp = _softplus(z)
            if offset is None:
                lom = -sp
            else:
                mask = (cs + offset) < rj
                lom = jnp.where(mask, -sp, 0.0)
            hi = lom.astype(BF16)
            lo = (lom - hi.astype(F32)).astype(BF16)
            between = _dot(hi, later) + _dot(lo, later) + runs[g]
            w = jnp.exp(z - sp + between)
            if offset is not None:
                w = jnp.where(mask, w, 0.0)
            new_accs.append(accs[g] + _dot(w.astype(BF16), v))
            new_runs.append(runs[g] + jnp.sum(lom, axis=1, keepdims=True))
        return new_accs, new_runs

    accs = [jnp.zeros((tq, hd), F32) for _ in range(hg)]
    runs = [jnp.zeros((tq, 1), F32) for _ in range(hg)]
    for j in range(nd):
        accs, runs = block(qi * nd + (nd - 1 - j), accs, runs, (nd - 1 - j) * tk)

    def run_max(rs):
        m = rs[0]
        for r in rs[1:]:
            m = jnp.maximum(m, r)
        return jnp.max(m)

    def cond(carry):
        i, top, _, _ = carry
        return jnp.logical_and(i < qi * nd, top > SB_DEAD_LOG)

    def body(carry):
        i, _, accs, runs = carry
        accs, runs = block(qi * nd - 1 - i, list(accs), list(runs), None)
        return i + 1, run_max(runs), tuple(accs), tuple(runs)

    _, _, accs, _ = lax.while_loop(cond, body, (jnp.int32(0), run_max(runs), tuple(accs), tuple(runs)))
    for g in range(hg):
        acc = accs[g]
        o = acc * lax.rsqrt(jnp.mean(acc * acc, axis=-1, keepdims=True) + RMS_EPS) * nw_ref[:, g * hd:(g + 1) * hd]
        o_ref[:, g * hd:(g + 1) * hd] = o.astype(BF16)


def _sb_call(proj, sb_norm_w, batch, seq, heads):
    t = proj.shape[0]
    tq = _pick(seq, (256, 128))
    tk = tq
    hg = 2
    nq = seq // tq
    hb = heads // hg
    wd = hg * SB_HEAD_DIM
    return pl.pallas_call(
        functools.partial(_sb_kernel, tq=tq, tk=tk, hg=hg),
        out_shape=jax.ShapeDtypeStruct((t, heads * SB_HEAD_DIM), BF16),
        grid=(batch, hb, nq),
        in_specs=[
            pl.BlockSpec((tq, wd), lambda b, h, i: (b * nq + i, h)),
            pl.BlockSpec((seq, wd), lambda b, h, i: (b, hb + h)),
            pl.BlockSpec((seq, wd), lambda b, h, i: (b, 2 * hb + h)),
            pl.BlockSpec((1, wd), lambda b, h, i: (0, h)),
        ],
        out_specs=pl.BlockSpec((tq, wd), lambda b, h, i: (b * nq + i, h)),
        compiler_params=_params(("arbitrary", "arbitrary", "arbitrary")),
        name="sb_attn",
    )(proj, proj, proj, sb_norm_w.reshape(1, -1))


def _ssd_kernel(z_ref, xs_ref, b_ref, c_ref, dt_ref, dtt_ref, cw_ref, cb_ref, dtb_ref, dtbt_ref,
                alog_ref, alogt_ref, dskip_ref, nw_ref, o_ref, prev_scr, state_scr, *, heads):
    ln = xs_ref.shape[0]
    w = xs_ref.shape[1]
    gn = b_ref.shape[1]
    n = gn // SSD_GROUPS
    wg = w // SSD_GROUPS
    hpg = heads // SSD_GROUPS

    @pl.when(pl.program_id(1) == 0)
    def _():
        prev_scr[...] = jnp.zeros_like(prev_scr)
        state_scr[...] = jnp.zeros_like(state_scr)

    raw = jnp.concatenate([xs_ref[...], b_ref[...], c_ref[...]], axis=1).astype(F32)
    pr = prev_scr[...]
    ch = raw.shape[1]
    row = lax.broadcasted_iota(jnp.int32, (ln, ch), 0)
    taps = cw_ref.shape[0]
    acc = raw * cw_ref[taps - 1:taps, :] + cb_ref[...]
    for sh in range(1, taps):
        shifted = jnp.where(row < sh, pltpu.roll(pr, sh, 0), pltpu.roll(raw, sh, 0))
        acc = acc + shifted * cw_ref[taps - 1 - sh:taps - sh, :]
    prev_scr[...] = raw
    xbc = _silu(acc)
    xs = xbc[:, :w]
    bm = xbc[:, w:w + gn]
    cm = xbc[:, w + gn:]

    dtv = _softplus(dt_ref[0] + dtb_ref[...])
    dtvt = _softplus(dtt_ref[0] + dtbt_ref[...])
    a_full = -jnp.exp(alog_ref[...])
    a_t = -jnp.exp(alogt_ref[...])

    eh = lax.broadcasted_iota(jnp.int32, (heads, w), 0)
    ec = lax.broadcasted_iota(jnp.int32, (heads, w), 1)
    expand = jnp.where(ec // SSD_HEAD_DIM == eh, 1.0, 0.0).astype(BF16)
    d_hi, d_mid, d_lo = _split3(dtv)
    dt_exp = _dot(d_hi, expand) + _dot(d_mid, expand) + _dot(d_lo, expand)

    rl = lax.broadcasted_iota(jnp.int32, (ln, ln), 0)
    cl = lax.broadcasted_iota(jnp.int32, (ln, ln), 1)
    tril = cl <= rl
    incl = jnp.where(tril, 1.0, 0.0).astype(BF16)
    incl_t = jnp.where(rl <= cl, 1.0, 0.0).astype(BF16)

    a_exp = dt_exp * a_full
    a_hi, a_mid, a_lo = _split3(a_exp)
    acum = _dot(incl, a_hi) + _dot(incl, a_mid) + _dot(incl, a_lo)
    at = dtvt * a_t
    t_hi, t_mid, t_lo = _split3(at)
    acum_t = _dot(t_hi, incl_t) + _dot(t_mid, incl_t) + _dot(t_lo, incl_t)

    xdt = xs * dt_exp
    last = acum[ln - 1:ln, :]
    xdd = (xdt * jnp.exp(last - acum)).astype(BF16)
    xdtb = xdt.astype(BF16)
    ea = jnp.exp(acum)
    cdec = jnp.exp(last)

    col_head = lax.broadcasted_iota(jnp.int32, (ln, wg), 1) // SSD_HEAD_DIM
    ys = []
    for g in range(SSD_GROUPS):
        bg = bm[:, g * n:(g + 1) * n].astype(BF16)
        cg = cm[:, g * n:(g + 1) * n].astype(BF16)
        cb = _dot_nt(cg, bg)
        ms = []
        for e in range(hpg):
            h = g * hpg + e
            ac = acum[:, h * SSD_HEAD_DIM:h * SSD_HEAD_DIM + 1]
            ar = acum_t[h:h + 1, :]
            lmat = jnp.exp(jnp.where(tril, ac - ar, -jnp.inf))
            ms.append((cb * lmat).astype(BF16))
        mcat = jnp.concatenate(ms, axis=1)
        xg = xdtb[:, g * wg:(g + 1) * wg]
        xbd = jnp.concatenate([jnp.where(col_head == e, xg, jnp.zeros_like(xg)) for e in range(hpg)], axis=0)
        y_diag = _dot(mcat, xbd)
        sg = state_scr[g]
        y_off = _dot(cg, sg.astype(BF16)) * ea[:, g * wg:(g + 1) * wg]
        s_new = _dot_tn(bg, xdd[:, g * wg:(g + 1) * wg])
        state_scr[g] = sg * cdec[:, g * wg:(g + 1) * wg] + s_new
        ys.append(y_diag + y_off)
    y = jnp.concatenate(ys, axis=1) + xs * dskip_ref[...]
    y = y * _silu(z_ref[...].astype(F32))
    o = y * lax.rsqrt(jnp.mean(y * y, axis=-1, keepdims=True) + RMS_EPS) * nw_ref[...]
    o_ref[...] = o.astype(BF16)


def _ssd_call(proj, dt, conv_w, conv_b, dt_bias, a_log, d_skip, ssd_norm_w, batch, seq, sbw):
    t = proj.shape[0]
    heads = dt_bias.shape[0]
    w = heads * SSD_HEAD_DIM
    gn = SSD_GROUPS * SSD_STATE
    ch = w + 2 * gn
    ln = SSD_CHUNK
    nc = seq // ln
    dt3 = dt.reshape(batch, seq, heads)
    dtt = jnp.swapaxes(dt3, 1, 2)
    rep = lambda v: jnp.repeat(v, SSD_HEAD_DIM).reshape(1, w)
    z_blk, xs_blk = 3 * sbw // w, (3 * sbw + w) // w
    b_blk, c_blk = (3 * sbw + 2 * w) // gn, (3 * sbw + 2 * w + gn) // gn
    assert 3 * sbw % w == 0 and (3 * sbw + 2 * w) % gn == 0
    full = lambda shape: pl.BlockSpec(shape, lambda b, c: (0,) * len(shape))
    return pl.pallas_call(
        functools.partial(_ssd_kernel, heads=heads),
        out_shape=jax.ShapeDtypeStruct((t, w), BF16),
        grid=(batch, nc),
        in_specs=[
            pl.BlockSpec((ln, w), lambda b, c: (b * nc + c, z_blk)),
            pl.BlockSpec((ln, w), lambda b, c: (b * nc + c, xs_blk)),
            pl.BlockSpec((ln, gn), lambda b, c: (b * nc + c, b_blk)),
            pl.BlockSpec((ln, gn), lambda b, c: (b * nc + c, c_blk)),
            pl.BlockSpec((1, ln, heads), lambda b, c: (b, c, 0)),
            pl.BlockSpec((1, heads, ln), lambda b, c: (b, 0, c)),
            full(conv_w.shape), full((1, ch)), full((1, heads)), full((heads, 1)),
            full((1, w)), full((heads, 1)), full((1, w)), full((1, w)),
        ],
        out_specs=pl.BlockSpec((ln, w), lambda b, c: (b * nc + c, 0)),
        scratch_shapes=[pltpu.VMEM((ln, ch), F32), pltpu.VMEM((SSD_GROUPS, SSD_STATE, w // SSD_GROUPS), F32)],
        compiler_params=_params(("arbitrary", "arbitrary")),
        name="ssd",
    )(proj, proj, proj, proj, dt3, dtt, conv_w, conv_b.reshape(1, ch), dt_bias.reshape(1, heads),
      dt_bias.reshape(heads, 1), rep(a_log), a_log.reshape(heads, 1), rep(d_skip), ssd_norm_w.reshape(1, w))


def _outproj_kernel(osb_ref, ossd_ref, x_ref, mod_ref, w1_ref, w2_ref, g_ref, b_ref, x1_ref, h2_ref, *, alpha):
    m = _dot(osb_ref[...], w1_ref[...]) + _dot(ossd_ref[...], w2_ref[...])
    v = alpha * x_ref[...] + (1.0 + mod_ref[0, 2:3, :]) * m
    x1 = _ln_stats(v) * g_ref[...] + b_ref[...]
    x1_ref[...] = x1
    h2_ref[...] = _ln_stats(x1) * (1.0 + mod_ref[0, 4:5, :]) + mod_ref[0, 3:4, :]


def _outproj_call(o_sb, o_ssd, x2d, mod, w_out_b, ln_g, ln_b, seq, alpha):
    t, d = x2d.shape
    sbw, ssw = o_sb.shape[1], o_ssd.shape[1]
    tm = _pick(seq, (256, 128))
    per_b = seq // tm
    return pl.pallas_call(
        functools.partial(_outproj_kernel, alpha=alpha),
        out_shape=(jax.ShapeDtypeStruct((t, d), F32), jax.ShapeDtypeStruct((t, d), F32)),
        grid=(t // tm,),
        in_specs=[
            pl.BlockSpec((tm, sbw), lambda i: (i, 0)),
            pl.BlockSpec((tm, ssw), lambda i: (i, 0)),
            pl.BlockSpec((tm, d), lambda i: (i, 0)),
            pl.BlockSpec((1, 6, d), lambda i: (i // per_b, 0, 0)),
            pl.BlockSpec((sbw, d), lambda i: (0, 0)),
            pl.BlockSpec((ssw, d), lambda i: (0, 0)),
            pl.BlockSpec((1, d), lambda i: (0, 0)),
            pl.BlockSpec((1, d), lambda i: (0, 0)),
        ],
        out_specs=(pl.BlockSpec((tm, d), lambda i: (i, 0)), pl.BlockSpec((tm, d), lambda i: (i, 0))),
        compiler_params=_params(("arbitrary",)),
        name="out_proj",
    )(o_sb, o_ssd, x2d, mod, w_out_b[:sbw], w_out_b[sbw:], ln_g.reshape(1, d), ln_b.reshape(1, d))


def _router_kernel(h_ref, wr_ref, bias_ref, idx_ref, rnk_ref, gate_ref, cnt_ref, carry_scr):
    ne = wr_ref.shape[0]
    tm = h_ref.shape[0]
    per_g = ne // N_EXPERT_GROUPS
    neg = -jnp.inf

    @pl.when(pl.program_id(0) == 0)
    def _():
        carry_scr[...] = jnp.zeros_like(carry_scr)

    logits = _dot_nt(wr_ref[...], h_ref[...].astype(BF16))
    scores = jax.nn.sigmoid(logits)
    biased = scores + bias_ref[...]
    s3 = scores.reshape(N_EXPERT_GROUPS, per_g, tm)
    b3 = biased.reshape(N_EXPERT_GROUPS, per_g, tm)
    shape3 = (N_EXPERT_GROUPS, per_g, tm)
    j_io = lax.broadcasted_iota(jnp.int32, shape3, 1)
    g_io = lax.broadcasted_iota(jnp.int32, shape3, 0)
    e_io = g_io * per_g + j_io

    m1 = jnp.max(b3, axis=1, keepdims=True)
    i1 = jnp.min(jnp.where(b3 == m1, j_io, per_g), axis=1, keepdims=True)
    m2 = jnp.max(jnp.where(j_io == i1, neg, b3), axis=1, keepdims=True)
    gs = m1 + m2
    gg = lax.broadcasted_iota(jnp.int32, gs.shape, 0)
    gmask = jnp.zeros(gs.shape, F32)
    cur = gs
    for _ in range(TOPK_GROUPS):
        m = jnp.max(cur, axis=0, keepdims=True)
        gi = jnp.min(jnp.where(cur == m, gg, N_EXPERT_GROUPS), axis=0, keepdims=True)
        pick = gg == gi
        gmask = jnp.where(pick, 1.0, gmask)
        cur = jnp.where(pick, neg, cur)
    cur = jnp.where(jnp.broadcast_to(gmask, shape3) > 0.5, b3, neg)

    sel = jnp.zeros(shape3, F32)
    picks = []
    for _ in range(TOP_K):
        m = jnp.max(jnp.max(cur, axis=1, keepdims=True), axis=0, keepdims=True)
        ei = jnp.min(jnp.min(jnp.where(cur == m, e_io, ne), axis=1, keepdims=True), axis=0, keepdims=True)
        pick = e_io == ei
        sel = jnp.where(pick, 1.0, sel)
        cur = jnp.where(pick, neg, cur)
        picks.append((pick, ei))

    selw = sel * s3
    denom = jnp.sum(jnp.sum(selw, axis=1, keepdims=True), axis=0, keepdims=True)
    gates3 = selw / denom * ROUTED_SCALE

    sel2 = sel.reshape(ne, tm)
    rj = lax.broadcasted_iota(jnp.int32, (tm, tm), 0)
    cs = lax.broadcasted_iota(jnp.int32, (tm, tm), 1)
    before = jnp.where(rj < cs, 1.0, 0.0).astype(BF16)
    carry = carry_scr[...]
    rank2 = _dot(sel2.astype(BF16), before) + carry[:, 0:1]
    rank3 = rank2.reshape(shape3)
    new_carry = carry + jnp.sum(sel2, axis=1, keepdims=True)
    carry_scr[...] = new_carry
    cnt_ref[...] = new_carry.astype(jnp.int32)

    for k, (pick, ei) in enumerate(picks):
        red = lambda v: jnp.sum(jnp.sum(v, axis=1, keepdims=True), axis=0, keepdims=True).reshape(1, tm)
        idx_ref[k:k + 1, :] = ei.reshape(1, tm)
        rnk_ref[k:k + 1, :] = red(jnp.where(pick, rank3, 0.0)).astype(jnp.int32)
        gate_ref[k:k + 1, :] = red(jnp.where(pick, gates3, 0.0))


def _router_call(h2, wr_t, bias):
    t, d = h2.shape
    ne = wr_t.shape[0]
    tm = _pick(t, (512, 256))
    if t == tm:
        tm = tm // 2
    return pl.pallas_call(
        _router_kernel,
        out_shape=(jax.ShapeDtypeStruct((TOP_K, t), jnp.int32), jax.ShapeDtypeStruct((TOP_K, t), jnp.int32),
                   jax.ShapeDtypeStruct((TOP_K, t), F32), jax.ShapeDtypeStruct((ne, 128), jnp.int32)),
        grid=(t // tm,),
        in_specs=[
            pl.BlockSpec((tm, d), lambda i: (i, 0)),
            pl.BlockSpec((ne, d), lambda i: (0, 0)),
            pl.BlockSpec((ne, 1), lambda i: (0, 0)),
        ],
        out_specs=(pl.BlockSpec((TOP_K, tm), lambda i: (0, i)), pl.BlockSpec((TOP_K, tm), lambda i: (0, i)),
                   pl.BlockSpec((TOP_K, tm), lambda i: (0, i)), pl.BlockSpec((ne, 128), lambda i: (0, 0))),
        scratch_shapes=[pltpu.VMEM((ne, 128), F32)],
        compiler_params=_params(("arbitrary",)),
        name="router",
    )(h2, wr_t, bias.reshape(ne, 1))


def _expert_kernel(be_ref, nu_ref, tok_ref, h_hbm, wg_ref, wu_ref, wd_ref, o_ref, xbuf, idx_smem, isem, gsem):
    tb = xbuf.shape[0]

    @pl.when(pl.program_id(0) < nu_ref[0])
    def _():
        cp = pltpu.make_async_copy(tok_ref.at[0, 0], idx_smem, isem)
        cp.start()
        cp.wait()

        def issue(i, c):
            pltpu.make_async_copy(h_hbm.at[idx_smem[i]], xbuf.at[i], gsem).start()
            return c

        lax.fori_loop(0, tb, issue, 0, unroll=8)
        pltpu.make_async_copy(h_hbm.at[pl.ds(0, tb)], xbuf, gsem).wait()
        x = xbuf[...].astype(BF16)
        a = (_silu(_dot(x, wg_ref[0])) * _dot(x, wu_ref[0])).astype(BF16)
        o_ref[...] = _dot(a, wd_ref[0])

    @pl.when(pl.program_id(0) >= nu_ref[0])
    def _():
        o_ref[...] = jnp.zeros_like(o_ref)


def _expert_call(block_expert, n_used, row_tok, h2, wg, wu, wd, tb):
    t, d = h2.shape
    ne, _, de = wg.shape
    nblk = row_tok.shape[0] // tb
    blk = lambda r, be, nu: jnp.minimum(r, nu[0] - 1)
    grid_spec = pltpu.PrefetchScalarGridSpec(
        num_scalar_prefetch=2,
        grid=(nblk,),
        in_specs=[
            pl.BlockSpec((1, 1, tb), lambda r, be, nu: (blk(r, be, nu), 0, 0)),
            pl.BlockSpec(memory_space=pl.ANY),
            pl.BlockSpec((1, d, de), lambda r, be, nu: (be[blk(r, be, nu)], 0, 0)),
            pl.BlockSpec((1, d, de), lambda r, be, nu: (be[blk(r, be, nu)], 0, 0)),
            pl.BlockSpec((1, de, d), lambda r, be, nu: (be[blk(r, be, nu)], 0, 0)),
        ],
        out_specs=pl.BlockSpec((tb, d), lambda r, be, nu: (r, 0)),
        scratch_shapes=[pltpu.VMEM((tb, d), F32), pltpu.SMEM((tb,), jnp.int32),
                        pltpu.SemaphoreType.DMA, pltpu.SemaphoreType.DMA],
    )
    return pl.pallas_call(
        _expert_kernel,
        out_shape=jax.ShapeDtypeStruct((nblk * tb, d), F32),
        grid_spec=grid_spec,
        compiler_params=_params(("arbitrary",)),
        name="experts",
    )(block_expert, n_used, row_tok.reshape(nblk, 1, tb), h2, wg, wu, wd)


def _combine_kernel(pos_ref, gate_ref, h_ref, x_ref, mod_ref, o_hbm, sg_ref, su_ref, sd_ref, g_ref, b_ref,
                    out_ref, gbuf, idx_smem, isem, gsem, *, alpha):
    tm = h_ref.shape[0]
    cp = pltpu.make_async_copy(pos_ref.at[0, 0], idx_smem, isem)
    cp.start()
    cp.wait()

    def issue(i, c):
        pltpu.make_async_copy(o_hbm.at[idx_smem[i]], gbuf.at[i], gsem).start()
        return c

    lax.fori_loop(0, TOP_K * tm, issue, 0, unroll=8)
    hb = h_ref[...].astype(BF16)
    a = (_silu(_dot(hb, sg_ref[...])) * _dot(hb, su_ref[...])).astype(BF16)
    f = _dot(a, sd_ref[...])
    pltpu.make_async_copy(o_hbm.at[pl.ds(0, TOP_K * tm)], gbuf, gsem).wait()
    gate = gate_ref[...]
    for k in range(TOP_K):
        f = f + gbuf[pl.ds(k * tm, tm), :] * gate[:, k:k + 1]
    v = alpha * x_ref[...] + (1.0 + mod_ref[0, 5:6, :]) * f
    out_ref[...] = _ln_stats(v) * g_ref[...] + b_ref[...]


def _combine_call(pos, gate_t, h2, x1, mod, o_sorted, sg, su, sd, ln_g, ln_b, seq, alpha):
    t, d = h2.shape
    ds = sg.shape[1]
    tm = 128
    per_b = seq // tm
    nt = t // tm
    return pl.pallas_call(
        functools.partial(_combine_kernel, alpha=alpha),
        out_shape=jax.ShapeDtypeStruct((t, d), F32),
        grid=(nt,),
        in_specs=[
            pl.BlockSpec((1, 1, TOP_K * tm), lambda i: (i, 0, 0)),
            pl.BlockSpec((tm, TOP_K), lambda i: (i, 0)),
            pl.BlockSpec((tm, d), lambda i: (i, 0)),
            pl.BlockSpec((tm, d), lambda i: (i, 0)),
            pl.BlockSpec((1, 6, d), lambda i: (i // per_b, 0, 0)),
            pl.BlockSpec(memory_space=pl.ANY),
            pl.BlockSpec((d, ds), lambda i: (0, 0)),
            pl.BlockSpec((d, ds), lambda i: (0, 0)),
            pl.BlockSpec((ds, d), lambda i: (0, 0)),
            pl.BlockSpec((1, d), lambda i: (0, 0)),
            pl.BlockSpec((1, d), lambda i: (0, 0)),
        ],
        out_specs=pl.BlockSpec((tm, d), lambda i: (i, 0)),
        scratch_shapes=[pltpu.VMEM((TOP_K * tm, d), F32), pltpu.SMEM((TOP_K * tm,), jnp.int32),
                        pltpu.SemaphoreType.DMA, pltpu.SemaphoreType.DMA],
        compiler_params=_params(("arbitrary",)),
        name="combine",
    )(pos, gate_t, h2, x1, mod, o_sorted, sg, su, sd, ln_g.reshape(1, d), ln_b.reshape(1, d))


def _moe(h2, x1, mod, wr_t, router_bias, wg, wu, wd, sg, su, sd, ln_g, ln_b, seq, alpha, tb):
    t, d = h2.shape
    ne = wg.shape[0]
    idx, rnk, gate, cnt = _router_call(h2, wr_t, router_bias)
    counts = cnt[:, 0]
    padded = (counts + tb - 1) // tb * tb
    pend = jnp.cumsum(padded)
    pstart = pend - padded
    eids = jnp.arange(ne, dtype=jnp.int32)
    dest = rnk + jnp.sum(jnp.where(idx[..., None] == eids, pstart.astype(jnp.int32), 0), axis=-1)
    nblk = (t * TOP_K) // tb + ne
    tok = jnp.broadcast_to(jnp.arange(t, dtype=jnp.int32)[None, :], (TOP_K, t))
    row_tok = jnp.zeros((nblk * tb,), jnp.int32).at[dest.reshape(-1)].set(tok.reshape(-1))
    blk_row = jnp.arange(nblk, dtype=jnp.int32) * tb
    block_expert = jnp.minimum(jnp.sum((pend[None, :] <= blk_row[:, None]).astype(jnp.int32), axis=1), ne - 1)
    n_used = (pend[-1] // tb).astype(jnp.int32).reshape(1)
    o_sorted = _expert_call(block_expert, n_used, row_tok, h2, wg, wu, wd, tb)
    tm = 128
    pos = dest.reshape(TOP_K, t // tm, tm).transpose(1, 0, 2).reshape(t // tm, 1, TOP_K * tm)
    return _combine_call(pos, gate.T, h2, x1, mod, o_sorted, sg, su, sd, ln_g, ln_b, seq, alpha)


def kernel(x, c, w_ada, b_ada, w_in, conv_w, conv_b, dt_bias, a_log, d_skip, sb_norm_w, ssd_norm_w, w_out, ln1_g,
           ln1_b, w_router, router_bias, w_gate, w_up, w_down, ws_gate, ws_up, ws_down, ln2_g, ln2_b):
    batch, seq, d = x.shape
    depth = w_ada.shape[0]
    sbw = sb_norm_w.shape[1]
    sb_heads = sbw // SB_HEAD_DIM
    ssd_heads = dt_bias.shape[1]
    n_main = w_in.shape[2] - ssd_heads
    alpha = (2 * depth) ** 0.25
    tb = 256 if (batch * seq * TOP_K) // w_gate.shape[1] >= 1024 else 128

    mod_all = _ada_call(c, w_ada, b_ada).reshape(depth, batch, 6, d)
    x2d = x.reshape(batch * seq, d)
    for l in range(depth):
        mod = mod_all[l]
        w_main = w_in[l, :, :n_main].astype(BF16)
        w_dt = jnp.pad(w_in[l, :, n_main:], ((0, 0), (0, 128 - ssd_heads))).astype(BF16)
        proj, dtp = _inproj_call(x2d, mod, w_main, w_dt, seq)
        o_sb = _sb_call(proj, sb_norm_w[l], batch, seq, sb_heads)
        o_ssd = _ssd_call(proj, dtp[:, :ssd_heads], conv_w[l], conv_b[l], dt_bias[l], a_log[l], d_skip[l],
                          ssd_norm_w[l], batch, seq, sbw)
        x1, h2 = _outproj_call(o_sb, o_ssd, x2d, mod, w_out[l].astype(BF16), ln1_g[l], ln1_b[l], seq, alpha)
        x2d = _moe(h2, x1, mod, w_router[l].T.astype(BF16), router_bias[l], w_gate[l].astype(BF16),
                   w_up[l].astype(BF16), w_down[l].astype(BF16), ws_gate[l].astype(BF16), ws_up[l].astype(BF16),
                   ws_down[l].astype(BF16), ln2_g[l], ln2_b[l], seq, alpha, tb)
    return x2d.reshape(batch, seq, d)
```

```python
import functools
import math

import jax
import jax.numpy as jnp
from jax import lax
from jax.experimental import pallas as pl
from jax.experimental.pallas import tpu as pltpu

F32 = jnp.float32
BF16 = jnp.bfloat16

SB_HEAD_DIM = 128
SSD_HEAD_DIM = 64
SSD_GROUPS = 2
SSD_STATE = 128
SSD_CHUNK = 128
TOP_K = 8
N_EXPERT_GROUPS = 8
TOPK_GROUPS = 4
ROUTED_SCALE = 2.5
LN_EPS = 1e-5
RMS_EPS = 1e-6
SB_DEAD_LOG = -110.0

VMEM_LIMIT_BYTES = 56 * 1024 * 1024


def _params(sem, vmem=VMEM_LIMIT_BYTES):
    return pltpu.CompilerParams(dimension_semantics=sem, vmem_limit_bytes=vmem)


def _pick(n, cands):
    for c in cands:
        if n % c == 0:
            return c
    raise ValueError(f"no tile in {cands} divides {n}")


def _softplus(z):
    return jnp.maximum(z, 0.0) + jnp.log(1.0 + jnp.exp(-jnp.abs(z)))


def _silu(v):
    return v * jax.nn.sigmoid(v)


def _split3(v):
    hi = v.astype(BF16)
    r1 = v - hi.astype(F32)
    mid = r1.astype(BF16)
    lo = (r1 - mid.astype(F32)).astype(BF16)
    return hi, mid, lo


def _dot(a, b):
    return jnp.dot(a, b, preferred_element_type=F32)


def _dot_nt(a, b):
    return lax.dot_general(a, b, (((1,), (1,)), ((), ())), preferred_element_type=F32)


def _dot_tn(a, b):
    return lax.dot_general(a, b, (((0,), (0,)), ((), ())), preferred_element_type=F32)


def _ln_stats(v):
    mu = jnp.mean(v, axis=-1, keepdims=True)
    d = v - mu
    var = jnp.mean(d * d, axis=-1, keepdims=True)
    return d * lax.rsqrt(var + LN_EPS)


def _ada_kernel(c_ref, w_ref, b_ref, o_ref):
    ca = _silu(c_ref[...]).astype(BF16)
    o_ref[0] = _dot(ca, w_ref[0].astype(BF16)) + b_ref[0]


def _ada_call(c, w_ada, b_ada):
    depth, d, n = w_ada.shape
    b = c.shape[0]
    tn = _pick(n, (512, 256, 128))
    return pl.pallas_call(
        _ada_kernel,
        out_shape=jax.ShapeDtypeStruct((depth, b, n), F32),
        grid=(depth, n // tn),
        in_specs=[
            pl.BlockSpec((b, d), lambda l, j: (0, 0)),
            pl.BlockSpec((1, d, tn), lambda l, j: (l, 0, j)),
            pl.BlockSpec((1, 1, tn), lambda l, j: (l, 0, j)),
        ],
        out_specs=pl.BlockSpec((1, b, tn), lambda l, j: (l, 0, j)),
        compiler_params=_params(("arbitrary", "arbitrary")),
        name="ada_mod",
    )(c, w_ada, b_ada.reshape(depth, 1, n))


def _inproj_kernel(x_ref, mod_ref, w_ref, wdt_ref, o_ref, dt_ref, h_scr):
    @pl.when(pl.program_id(1) == 0)
    def _():
        xn = _ln_stats(x_ref[...])
        h = xn * (1.0 + mod_ref[0, 1:2, :]) + mod_ref[0, 0:1, :]
        hb = h.astype(BF16)
        h_scr[...] = hb
        dt_ref[...] = _dot(hb, wdt_ref[...])

    o_ref[...] = _dot(h_scr[...], w_ref[...]).astype(BF16)


def _inproj_call(x2d, mod, w_main, w_dt, seq):
    t, d = x2d.shape
    n = w_main.shape[1]
    tm = _pick(seq, (1024, 512, 256, 128))
    tn = _pick(n, (512, 256, 128))
    per_b = seq // tm
    return pl.pallas_call(
        _inproj_kernel,
        out_shape=(jax.ShapeDtypeStruct((t, n), BF16), jax.ShapeDtypeStruct((t, 128), F32)),
        grid=(t // tm, n // tn),
        in_specs=[
            pl.BlockSpec((tm, d), lambda i, j: (i, 0)),
            pl.BlockSpec((1, 6, d), lambda i, j: (i // per_b, 0, 0)),
            pl.BlockSpec((d, tn), lambda i, j: (0, j)),
            pl.BlockSpec((d, 128), lambda i, j: (0, 0)),
        ],
        out_specs=(
            pl.BlockSpec((tm, tn), lambda i, j: (i, j)),
            pl.BlockSpec((tm, 128), lambda i, j: (i, 0)),
        ),
        scratch_shapes=[pltpu.VMEM((tm, d), BF16)],
        compiler_params=_params(("arbitrary", "arbitrary")),
        name="in_proj",
    )(x2d, mod, w_main, w_dt)


def _sb_kernel(q_ref, k_ref, v_ref, nw_ref, o_ref, *, tq, tk, hg):
    qi = pl.program_id(2)
    hd = SB_HEAD_DIM
    scale = 1.0 / math.sqrt(hd)
    nd = tq // tk
    rj = lax.broadcasted_iota(jnp.int32, (tq, tk), 0)
    cs = lax.broadcasted_iota(jnp.int32, (tq, tk), 1)
    lr = lax.broadcasted_iota(jnp.int32, (tk, tk), 0)
    lc = lax.broadcasted_iota(jnp.int32, (tk, tk), 1)
    later = jnp.where(lr > lc, 1.0, 0.0).astype(BF16)
    qs = [q_ref[:, g * hd:(g + 1) * hd] for g in range(hg)]

    def block(kb, accs, runs, offset):
        start = pl.multiple_of(kb * tk, tk)
        new_accs, new_runs = [], []
        for g in range(hg):
            k = k_ref[pl.ds(start, tk), g * hd:(g + 1) * hd]
            v = v_ref[pl.ds(start, tk), g * hd:(g + 1) * hd]
            z = _dot_nt(qs[g], k) * scale
            sp = _softplus(z)
            if offset is None:
                lom = -sp
            else:
                mask = (cs + offset) < rj
                lom = jnp.where(mask, -sp, 0.0)
            hi = lom.astype(BF16)
            lo = (lom - hi.astype(F32)).astype(BF16)
            between = _dot(hi, later) + _dot(lo, later) + runs[g]
            w = jnp.exp(z - sp + between)
            if offset is not None:
                w = jnp.where(mask, w, 0.0)
            new_accs.append(accs[g] + _dot(w.astype(BF16), v))
            new_runs.append(runs[g] + jnp.sum(lom, axis=1, keepdims=True))
        return new_accs, new_runs

    accs = [jnp.zeros((tq, hd), F32) for _ in range(hg)]
    runs = [jnp.zeros((tq, 1), F32) for _ in range(hg)]
    for j in range(nd):
        accs, runs = block(qi * nd + (nd - 1 - j), accs, runs, (nd - 1 - j) * tk)

    def run_max(rs):
        m = rs[0]
        for r in rs[1:]:
            m = jnp.maximum(m, r)
        return jnp.max(m)

    def cond(carry):
        i, top, _, _ = carry
        return jnp.logical_and(i < qi * nd, top > SB_DEAD_LOG)

    def body(carry):
        i, _, accs, runs = carry
        accs, runs = block(qi * nd - 1 - i, list(accs), list(runs), None)
        return i + 1, run_max(runs), tuple(accs), tuple(runs)

    _, _, accs, _ = lax.while_loop(cond, body, (jnp.int32(0), run_max(runs), tuple(accs), tuple(runs)))
    for g in range(hg):
        acc = accs[g]
        o = acc * lax.rsqrt(jnp.mean(acc * acc, axis=-1, keepdims=True) + RMS_EPS) * nw_ref[:, g * hd:(g + 1) * hd]
        o_ref[:, g * hd:(g + 1) * hd] = o.astype(BF16)


def _sb_call(proj, sb_norm_w, batch, seq, heads):
    t = proj.shape[0]
    tq = _pick(seq, (256, 128))
    tk = tq
    hg = 2
    nq = seq // tq
    hb = heads // hg
    wd = hg * SB_HEAD_DIM
    return pl.pallas_call(
        functools.partial(_sb_kernel, tq=tq, tk=tk, hg=hg),
        out_shape=jax.ShapeDtypeStruct((t, heads * SB_HEAD_DIM), BF16),
        grid=(batch, hb, nq),
        in_specs=[
            pl.BlockSpec((tq, wd), lambda b, h, i: (b * nq + i, h)),
            pl.BlockSpec((seq, wd), lambda b, h, i: (b, hb + h)),
            pl.BlockSpec((seq, wd), lambda b, h, i: (b, 2 * hb + h)),
            pl.BlockSpec((1, wd), lambda b, h, i: (0, h)),
        ],
        out_specs=pl.BlockSpec((tq, wd), lambda b, h, i: (b * nq + i, h)),
        compiler_params=_params(("arbitrary", "arbitrary", "arbitrary")),
        name="sb_attn",
    )(proj, proj, proj, sb_norm_w.reshape(1, -1))


def _ssd_kernel(z_ref, xs_ref, b_ref, c_ref, dt_ref, dtt_ref, cw_ref, cb_ref, dtb_ref, dtbt_ref,
                alog_ref, alogt_ref, dskip_ref, nw_ref, o_ref, prev_scr, state_scr, *, heads):
    ln = xs_ref.shape[0]
    w = xs_ref.shape[1]
    gn = b_ref.shape[1]
    n = gn // SSD_GROUPS
    wg = w // SSD_GROUPS
    hpg = heads // SSD_GROUPS

    @pl.when(pl.program_id(1) == 0)
    def _():
        prev_scr[...] = jnp.zeros_like(prev_scr)
        state_scr[...] = jnp.zeros_like(state_scr)

    raw = jnp.concatenate([xs_ref[...], b_ref[...], c_ref[...]], axis=1).astype(F32)
    pr = prev_scr[...]
    ch = raw.shape[1]
    row = lax.broadcasted_iota(jnp.int32, (ln, ch), 0)
    taps = cw_ref.shape[0]
    acc = raw * cw_ref[taps - 1:taps, :] + cb_ref[...]
    for sh in range(1, taps):
        shifted = jnp.where(row < sh, pltpu.roll(pr, sh, 0), pltpu.roll(raw, sh, 0))
        acc = acc + shifted * cw_ref[taps - 1 - sh:taps - sh, :]
    prev_scr[...] = raw
    xbc = _silu(acc)
    xs = xbc[:, :w]
    bm = xbc[:, w:w + gn]
    cm = xbc[:, w + gn:]

    dtv = _softplus(dt_ref[0] + dtb_ref[...])
    dtvt = _softplus(dtt_ref[0] + dtbt_ref[...])
    a_full = -jnp.exp(alog_ref[...])
    a_t = -jnp.exp(alogt_ref[...])

    eh = lax.broadcasted_iota(jnp.int32, (heads, w), 0)
    ec = lax.broadcasted_iota(jnp.int32, (heads, w), 1)
    expand = jnp.where(ec // SSD_HEAD_DIM == eh, 1.0, 0.0).astype(BF16)
    d_hi, d_mid, d_lo = _split3(dtv)
    dt_exp = _dot(d_hi, expand) + _dot(d_mid, expand) + _dot(d_lo, expand)

    rl = lax.broadcasted_iota(jnp.int32, (ln, ln), 0)
    cl = lax.broadcasted_iota(jnp.int32, (ln, ln), 1)
    tril = cl <= rl
    incl = jnp.where(tril, 1.0, 0.0).astype(BF16)
    incl_t = jnp.where(rl <= cl, 1.0, 0.0).astype(BF16)

    a_exp = dt_exp * a_full
    a_hi, a_mid, a_lo = _split3(a_exp)
    acum = _dot(incl, a_hi) + _dot(incl, a_mid) + _dot(incl, a_lo)
    at = dtvt * a_t
    t_hi, t_mid, t_lo = _split3(at)
    acum_t = _dot(t_hi, incl_t) + _dot(t_mid, incl_t) + _dot(t_lo, incl_t)

    xdt = xs * dt_exp
    last = acum[ln - 1:ln, :]
    xdd = (xdt * jnp.exp(last - acum)).astype(BF16)
    xdtb = xdt.astype(BF16)
    ea = jnp.exp(acum)
    cdec = jnp.exp(last)

    col_head = lax.broadcasted_iota(jnp.int32, (ln, wg), 1) // SSD_HEAD_DIM
    ys = []
    for g in range(SSD_GROUPS):
        bg = bm[:, g * n:(g + 1) * n].astype(BF16)
        cg = cm[:, g * n:(g + 1) * n].astype(BF16)
        cb = _dot_nt(cg, bg)
        ms = []
        for e in range(hpg):
            h = g * hpg + e
            ac = acum[:, h * SSD_HEAD_DIM:h * SSD_HEAD_DIM + 1]
            ar = acum_t[h:h + 1, :]
            lmat = jnp.exp(jnp.where(tril, ac - ar, -jnp.inf))
            ms.append((cb * lmat).astype(BF16))
        mcat = jnp.concatenate(ms, axis=1)
        xg = xdtb[:, g * wg:(g + 1) * wg]
        xbd = jnp.concatenate([jnp.where(col_head == e, xg, jnp.zeros_like(xg)) for e in range(hpg)], axis=0)
        y_diag = _dot(mcat, xbd)
        sg = state_scr[g]
        y_off = _dot(cg, sg.astype(BF16)) * ea[:, g * wg:(g + 1) * wg]
        s_new = _dot_tn(bg, xdd[:, g * wg:(g + 1) * wg])
        state_scr[g] = sg * cdec[:, g * wg:(g + 1) * wg] + s_new
        ys.append(y_diag + y_off)
    y = jnp.concatenate(ys, axis=1) + xs * dskip_ref[...]
    y = y * _silu(z_ref[...].astype(F32))
    o = y * lax.rsqrt(jnp.mean(y * y, axis=-1, keepdims=True) + RMS_EPS) * nw_ref[...]
    o_ref[...] = o.astype(BF16)


def _ssd_call(proj, dt, conv_w, conv_b, dt_bias, a_log, d_skip, ssd_norm_w, batch, seq, sbw):
    t = proj.shape[0]
    heads = dt_bias.shape[0]
    w = heads * SSD_HEAD_DIM
    gn = SSD_GROUPS * SSD_STATE
    ch = w + 2 * gn
    ln = SSD_CHUNK
    nc = seq // ln
    dt3 = dt.reshape(batch, seq, heads)
    dtt = jnp.swapaxes(dt3, 1, 2)
    rep = lambda v: jnp.repeat(v, SSD_HEAD_DIM).reshape(1, w)
    z_blk, xs_blk = 3 * sbw // w, (3 * sbw + w) // w
    b_blk, c_blk = (3 * sbw + 2 * w) // gn, (3 * sbw + 2 * w + gn) // gn
    assert 3 * sbw % w == 0 and (3 * sbw + 2 * w) % gn == 0
    full = lambda shape: pl.BlockSpec(shape, lambda b, c: (0,) * len(shape))
    return pl.pallas_call(
        functools.partial(_ssd_kernel, heads=heads),
        out_shape=jax.ShapeDtypeStruct((t, w), BF16),
        grid=(batch, nc),
        in_specs=[
            pl.BlockSpec((ln, w), lambda b, c: (b * nc + c, z_blk)),
            pl.BlockSpec((ln, w), lambda b, c: (b * nc + c, xs_blk)),
            pl.BlockSpec((ln, gn), lambda b, c: (b * nc + c, b_blk)),
            pl.BlockSpec((ln, gn), lambda b, c: (b * nc + c, c_blk)),
            pl.BlockSpec((1, ln, heads), lambda b, c: (b, c, 0)),
            pl.BlockSpec((1, heads, ln), lambda b, c: (b, 0, c)),
            full(conv_w.shape), full((1, ch)), full((1, heads)), full((heads, 1)),
            full((1, w)), full((heads, 1)), full((1, w)), full((1, w)),
        ],
        out_specs=pl.BlockSpec((ln, w), lambda b, c: (b * nc + c, 0)),
        scratch_shapes=[pltpu.VMEM((ln, ch), F32), pltpu.VMEM((SSD_GROUPS, SSD_STATE, w // SSD_GROUPS), F32)],
        compiler_params=_params(("arbitrary", "arbitrary")),
        name="ssd",
    )(proj, proj, proj, proj, dt3, dtt, conv_w, conv_b.reshape(1, ch), dt_bias.reshape(1, heads),
      dt_bias.reshape(heads, 1), rep(a_log), a_log.reshape(heads, 1), rep(d_skip), ssd_norm_w.reshape(1, w))


def _outproj_kernel(osb_ref, ossd_ref, x_ref, mod_ref, w1_ref, w2_ref, g_ref, b_ref, x1_ref, h2_ref, *, alpha):
    m = _dot(osb_ref[...], w1_ref[...]) + _dot(ossd_ref[...], w2_ref[...])
    v = alpha * x_ref[...] + (1.0 + mod_ref[0, 2:3, :]) * m
    x1 = _ln_stats(v) * g_ref[...] + b_ref[...]
    x1_ref[...] = x1
    h2_ref[...] = _ln_stats(x1) * (1.0 + mod_ref[0, 4:5, :]) + mod_ref[0, 3:4, :]


def _outproj_call(o_sb, o_ssd, x2d, mod, w_out_b, ln_g, ln_b, seq, alpha):
    t, d = x2d.shape
    sbw, ssw = o_sb.shape[1], o_ssd.shape[1]
    tm = _pick(seq, (256, 128))
    per_b = seq // tm
    return pl.pallas_call(
        functools.partial(_outproj_kernel, alpha=alpha),
        out_shape=(jax.ShapeDtypeStruct((t, d), F32), jax.ShapeDtypeStruct((t, d), F32)),
        grid=(t // tm,),
        in_specs=[
            pl.BlockSpec((tm, sbw), lambda i: (i, 0)),
            pl.BlockSpec((tm, ssw), lambda i: (i, 0)),
            pl.BlockSpec((tm, d), lambda i: (i, 0)),
            pl.BlockSpec((1, 6, d), lambda i: (i // per_b, 0, 0)),
            pl.BlockSpec((sbw, d), lambda i: (0, 0)),
            pl.BlockSpec((ssw, d), lambda i: (0, 0)),
            pl.BlockSpec((1, d), lambda i: (0, 0)),
            pl.BlockSpec((1, d), lambda i: (0, 0)),
        ],
        out_specs=(pl.BlockSpec((tm, d), lambda i: (i, 0)), pl.BlockSpec((tm, d), lambda i: (i, 0))),
        compiler_params=_params(("arbitrary",)),
        name="out_proj",
    )(o_sb, o_ssd, x2d, mod, w_out_b[:sbw], w_out_b[sbw:], ln_g.reshape(1, d), ln_b.reshape(1, d))


def _router_kernel(h_ref, wr_ref, bias_ref, idx_ref, rnk_ref, gate_ref, cnt_ref, carry_scr):
    ne = wr_ref.shape[0]
    tm = h_ref.shape[0]
    per_g = ne // N_EXPERT_GROUPS
    neg = -jnp.inf

    @pl.when(pl.program_id(0) == 0)
    def _():
        carry_scr[...] = jnp.zeros_like(carry_scr)

    logits = _dot_nt(wr_ref[...], h_ref[...].astype(BF16))
    scores = jax.nn.sigmoid(logits)
    biased = scores + bias_ref[...]
    s3 = scores.reshape(N_EXPERT_GROUPS, per_g, tm)
    b3 = biased.reshape(N_EXPERT_GROUPS, per_g, tm)
    shape3 = (N_EXPERT_GROUPS, per_g, tm)
    j_io = lax.broadcasted_iota(jnp.int32, shape3, 1)
    g_io = lax.broadcasted_iota(jnp.int32, shape3, 0)
    e_io = g_io * per_g + j_io

    m1 = jnp.max(b3, axis=1, keepdims=True)
    i1 = jnp.min(jnp.where(b3 == m1, j_io, per_g), axis=1, keepdims=True)
    m2 = jnp.max(jnp.where(j_io == i1, neg, b3), axis=1, keepdims=True)
    gs = m1 + m2
    gg = lax.broadcasted_iota(jnp.int32, gs.shape, 0)
    gmask = jnp.zeros(gs.shape, F32)
    cur = gs
    for _ in range(TOPK_GROUPS):
        m = jnp.max(cur, axis=0, keepdims=True)
        gi = jnp.min(jnp.where(cur == m, gg, N_EXPERT_GROUPS), axis=0, keepdims=True)
        pick = gg == gi
        gmask = jnp.where(pick, 1.0, gmask)
        cur = jnp.where(pick, neg, cur)
    cur = jnp.where(jnp.broadcast_to(gmask, shape3) > 0.5, b3, neg)

    sel = jnp.zeros(shape3, F32)
    picks = []
    for _ in range(TOP_K):
        m = jnp.max(jnp.max(cur, axis=1, keepdims=True), axis=0, keepdims=True)
        ei = jnp.min(jnp.min(jnp.where(cur == m, e_io, ne), axis=1, keepdims=True), axis=0, keepdims=True)
        pick = e_io == ei
        sel = jnp.where(pick, 1.0, sel)
        cur = jnp.where(pick, neg, cur)
        picks.append((pick, ei))

    selw = sel * s3
    denom = jnp.sum(jnp.sum(selw, axis=1, keepdims=True), axis=0, keepdims=True)
    gates3 = selw / denom * ROUTED_SCALE

    sel2 = sel.reshape(ne, tm)
    rj = lax.broadcasted_iota(jnp.int32, (tm, tm), 0)
    cs = lax.broadcasted_iota(jnp.int32, (tm, tm), 1)
    before = jnp.where(rj < cs, 1.0, 0.0).astype(BF16)
    carry = carry_scr[...]
    rank2 = _dot(sel2.astype(BF16), before) + carry[:, 0:1]
    rank3 = rank2.reshape(shape3)
    new_carry = carry + jnp.sum(sel2, axis=1, keepdims=True)
    carry_scr[...] = new_carry
    cnt_ref[...] = new_carry.astype(jnp.int32)

    for k, (pick, ei) in enumerate(picks):
        red = lambda v: jnp.sum(jnp.sum(v, axis=1, keepdims=True), axis=0, keepdims=True).reshape(1, tm)
        idx_ref[k:k + 1, :] = ei.reshape(1, tm)
        rnk_ref[k:k + 1, :] = red(jnp.where(pick, rank3, 0.0)).astype(jnp.int32)
        gate_ref[k:k + 1, :] = red(jnp.where(pick, gates3, 0.0))


def _router_call(h2, wr_t, bias):
    t, d = h2.shape
    ne = wr_t.shape[0]
    tm = _pick(t, (512, 256))
    if t == tm:
        tm = tm // 2
    return pl.pallas_call(
        _router_kernel,
        out_shape=(jax.ShapeDtypeStruct((TOP_K, t), jnp.int32), jax.ShapeDtypeStruct((TOP_K, t), jnp.int32),
                   jax.ShapeDtypeStruct((TOP_K, t), F32), jax.ShapeDtypeStruct((ne, 128), jnp.int32)),
        grid=(t // tm,),
        in_specs=[
            pl.BlockSpec((tm, d), lambda i: (i, 0)),
            pl.BlockSpec((ne, d), lambda i: (0, 0)),
            pl.BlockSpec((ne, 1), lambda i: (0, 0)),
        ],
        out_specs=(pl.BlockSpec((TOP_K, tm), lambda i: (0, i)), pl.BlockSpec((TOP_K, tm), lambda i: (0, i)),
                   pl.BlockSpec((TOP_K, tm), lambda i: (0, i)), pl.BlockSpec((ne, 128), lambda i: (0, 0))),
        scratch_shapes=[pltpu.VMEM((ne, 128), F32)],
        compiler_params=_params(("arbitrary",)),
        name="router",
    )(h2, wr_t, bias.reshape(ne, 1))


def _dispatch_kernel(ps_ref, pc_ref, nu_ref, dest_ref, h_ref, xs_hbm, zbuf, idx_smem, isem, ssem, zsem, tsem):
    tm = h_ref.shape[0]
    ne = ps_ref.shape[0]
    tb = zbuf.shape[0]
    nblk = xs_hbm.shape[0] // tb

    @pl.when(pl.program_id(0) == 0)
    def _():
        zbuf[...] = jnp.zeros_like(zbuf)

        def tail(r, c):
            pltpu.make_async_copy(zbuf, xs_hbm.at[pl.ds(pl.multiple_of(r * tb, tb), tb)], tsem).start()
            return c

        def tail_drain(r, c):
            pltpu.make_async_copy(zbuf, xs_hbm.at[pl.ds(0, tb)], tsem).wait()
            return c

        lax.fori_loop(nu_ref[0], nblk, tail, 0)
        lax.fori_loop(nu_ref[0], nblk, tail_drain, 0)

        def fill(e, c):
            def one(j, c2):
                pltpu.make_async_copy(zbuf.at[0], xs_hbm.at[ps_ref[e] + j], zsem).start()
                return c2
            return lax.fori_loop(0, pc_ref[e], one, c)

        def drain(e, c):
            def one(j, c2):
                pltpu.make_async_copy(zbuf.at[0], xs_hbm.at[0], zsem).wait()
                return c2
            return lax.fori_loop(0, pc_ref[e], one, c)

        lax.fori_loop(0, ne, fill, 0)
        lax.fori_loop(0, ne, drain, 0)

    cp = pltpu.make_async_copy(dest_ref.at[0, 0], idx_smem, isem)
    cp.start()
    cp.wait()

    def issue(i, c):
        for k in range(TOP_K):
            pltpu.make_async_copy(h_ref.at[i], xs_hbm.at[idx_smem[k * tm + i]], ssem).start()
        return c

    lax.fori_loop(0, tm, issue, 0, unroll=2)
    pltpu.make_async_copy(xs_hbm.at[pl.ds(0, TOP_K * tm)], xs_hbm.at[pl.ds(0, TOP_K * tm)], ssem).wait()


def _dispatch_call(pad_start, pad_cnt, n_used, dest, h2, n_rows, tb):
    t, d = h2.shape
    tm = _pick(t, (512, 256))
    nt = t // tm
    dest_t = dest.reshape(TOP_K, nt, tm).transpose(1, 0, 2).reshape(nt, 1, TOP_K * tm)
    grid_spec = pltpu.PrefetchScalarGridSpec(
        num_scalar_prefetch=3,
        grid=(nt,),
        in_specs=[
            pl.BlockSpec((1, 1, TOP_K * tm), lambda i, ps, pc, nu: (i, 0, 0)),
            pl.BlockSpec((tm, d), lambda i, ps, pc, nu: (i, 0)),
        ],
        out_specs=pl.BlockSpec(memory_space=pl.ANY),
        scratch_shapes=[pltpu.VMEM((tb, d), F32), pltpu.SMEM((TOP_K * tm,), jnp.int32), pltpu.SemaphoreType.DMA,
                        pltpu.SemaphoreType.DMA, pltpu.SemaphoreType.DMA, pltpu.SemaphoreType.DMA],
    )
    return pl.pallas_call(
        _dispatch_kernel,
        out_shape=jax.ShapeDtypeStruct((n_rows, d), F32),
        grid_spec=grid_spec,
        compiler_params=_params(("arbitrary",)),
        name="dispatch",
    )(pad_start, pad_cnt, n_used, dest_t, h2)


def _expert_kernel(be_ref, nu_ref, x_ref, wg_ref, wu_ref, wd_ref, o_ref, wgb, wub, wdb):
    r = pl.program_id(0)

    @pl.when(r < nu_ref[0])
    def _():
        @pl.when(jnp.logical_or(r == 0, be_ref[r] != be_ref[jnp.maximum(r - 1, 0)]))
        def _():
            wgb[...] = wg_ref[0, 0].astype(BF16)
            wub[...] = wu_ref[0, 0].astype(BF16)
            wdb[...] = wd_ref[0, 0].astype(BF16)

        x = x_ref[...].astype(BF16)
        a = (_silu(_dot(x, wgb[...])) * _dot(x, wub[...])).astype(BF16)
        o_ref[...] = _dot(a, wdb[...])

    @pl.when(r >= nu_ref[0])
    def _():
        o_ref[...] = jnp.zeros_like(o_ref)


def _expert_call(block_expert, n_used, xs, w_gate, w_up, w_down, layer, tb):
    n_rows, d = xs.shape
    de = w_gate.shape[3]
    nblk = n_rows // tb
    blk = lambda r, be, nu: jnp.minimum(r, nu[0] - 1)
    wmap = lambda r, be, nu: (layer, be[blk(r, be, nu)], 0, 0)
    grid_spec = pltpu.PrefetchScalarGridSpec(
        num_scalar_prefetch=2,
        grid=(nblk,),
        in_specs=[
            pl.BlockSpec((tb, d), lambda r, be, nu: (blk(r, be, nu), 0)),
            pl.BlockSpec((1, 1, d, de), wmap),
            pl.BlockSpec((1, 1, d, de), wmap),
            pl.BlockSpec((1, 1, de, d), wmap),
        ],
        out_specs=pl.BlockSpec((tb, d), lambda r, be, nu: (r, 0)),
        scratch_shapes=[pltpu.VMEM((d, de), BF16), pltpu.VMEM((d, de), BF16), pltpu.VMEM((de, d), BF16)],
    )
    return pl.pallas_call(
        _expert_kernel,
        out_shape=jax.ShapeDtypeStruct((n_rows, d), F32),
        grid_spec=grid_spec,
        compiler_params=_params(("arbitrary",)),
        name="experts",
    )(block_expert, n_used, xs, w_gate, w_up, w_down)


def _combine_kernel(pos_ref, posn_ref, gate_ref, h_ref, x_ref, mod_ref, o_hbm, sg_ref, su_ref, sd_ref, g_ref, b_ref,
                    out_ref, gbuf, idx_smem, isem, gsem, *, alpha):
    i = pl.program_id(0)
    tm = h_ref.shape[0]
    n = TOP_K * tm

    def gather(src_ref, slot):
        cp = pltpu.make_async_copy(src_ref.at[0, 0], idx_smem, isem)
        cp.start()
        cp.wait()

        def issue(j, c):
            pltpu.make_async_copy(o_hbm.at[idx_smem[j]], gbuf.at[slot, j], gsem.at[slot]).start()
            return c

        lax.fori_loop(0, n, issue, 0, unroll=8)

    @pl.when(i == 0)
    def _():
        gather(pos_ref, 0)

    @pl.when(i + 1 < pl.num_programs(0))
    def _():
        gather(posn_ref, (i + 1) % 2)

    hb = h_ref[...].astype(BF16)
    a = (_silu(_dot(hb, sg_ref[...])) * _dot(hb, su_ref[...])).astype(BF16)
    f = _dot(a, sd_ref[...])
    slot = i % 2
    pltpu.make_async_copy(o_hbm.at[pl.ds(0, n)], gbuf.at[slot], gsem.at[slot]).wait()
    gate = gate_ref[...]
    for k in range(TOP_K):
        f = f + gbuf[slot, pl.ds(k * tm, tm), :] * gate[:, k:k + 1]
    v = alpha * x_ref[...] + (1.0 + mod_ref[0, 5:6, :]) * f
    out_ref[...] = _ln_stats(v) * g_ref[...] + b_ref[...]


def _combine_call(pos, gate_t, h2, x1, mod, o_sorted, sg, su, sd, ln_g, ln_b, seq, alpha):
    t, d = h2.shape
    ds = sg.shape[1]
    tm = 128
    per_b = seq // tm
    nt = t // tm
    return pl.pallas_call(
        functools.partial(_combine_kernel, alpha=alpha),
        out_shape=jax.ShapeDtypeStruct((t, d), F32),
        grid=(nt,),
        in_specs=[
            pl.BlockSpec((1, 1, TOP_K * tm), lambda i: (i, 0, 0)),
            pl.BlockSpec((1, 1, TOP_K * tm), lambda i: (jnp.minimum(i + 1, nt - 1), 0, 0)),
            pl.BlockSpec((tm, TOP_K), lambda i: (i, 0)),
            pl.BlockSpec((tm, d), lambda i: (i, 0)),
            pl.BlockSpec((tm, d), lambda i: (i, 0)),
            pl.BlockSpec((1, 6, d), lambda i: (i // per_b, 0, 0)),
            pl.BlockSpec(memory_space=pl.ANY),
            pl.BlockSpec((d, ds), lambda i: (0, 0)),
            pl.BlockSpec((d, ds), lambda i: (0, 0)),
            pl.BlockSpec((ds, d), lambda i: (0, 0)),
            pl.BlockSpec((1, d), lambda i: (0, 0)),
            pl.BlockSpec((1, d), lambda i: (0, 0)),
        ],
        out_specs=pl.BlockSpec((tm, d), lambda i: (i, 0)),
        scratch_shapes=[pltpu.VMEM((2, TOP_K * tm, d), F32), pltpu.SMEM((TOP_K * tm,), jnp.int32),
                        pltpu.SemaphoreType.DMA, pltpu.SemaphoreType.DMA((2,))],
        compiler_params=_params(("arbitrary",)),
        name="combine",
    )(pos, pos, gate_t, h2, x1, mod, o_sorted, sg, su, sd, ln_g.reshape(1, d), ln_b.reshape(1, d))


def _moe(h2, x1, mod, wr_t, router_bias, w_gate, w_up, w_down, layer, sg, su, sd, ln_g, ln_b, seq, alpha, tb):
    t, d = h2.shape
    ne = w_gate.shape[1]
    idx, rnk, gate, cnt = _router_call(h2, wr_t, router_bias)
    counts = cnt[:, 0]
    padded = (counts + tb - 1) // tb * tb
    pend = jnp.cumsum(padded)
    pstart = pend - padded
    eids = jnp.arange(ne, dtype=jnp.int32)
    dest = rnk + jnp.sum(jnp.where(idx[..., None] == eids, pstart.astype(jnp.int32), 0), axis=-1)
    nblk = (t * TOP_K) // tb + ne
    blk_row = jnp.arange(nblk, dtype=jnp.int32) * tb
    block_expert = jnp.minimum(jnp.sum((pend[None, :] <= blk_row[:, None]).astype(jnp.int32), axis=1), ne - 1)
    n_used = (pend[-1] // tb).astype(jnp.int32).reshape(1)
    xs = _dispatch_call((pstart + counts).astype(jnp.int32), (padded - counts).astype(jnp.int32), n_used, dest, h2,
                        nblk * tb, tb)
    o_sorted = _expert_call(block_expert, n_used, xs, w_gate, w_up, w_down, layer, tb)
    tm = 128
    pos = dest.reshape(TOP_K, t // tm, tm).transpose(1, 0, 2).reshape(t // tm, 1, TOP_K * tm)
    return _combine_call(pos, gate.T, h2, x1, mod, o_sorted, sg, su, sd, ln_g, ln_b, seq, alpha)


def kernel(x, c, w_ada, b_ada, w_in, conv_w, conv_b, dt_bias, a_log, d_skip, sb_norm_w, ssd_norm_w, w_out, ln1_g,
           ln1_b, w_router, router_bias, w_gate, w_up, w_down, ws_gate, ws_up, ws_down, ln2_g, ln2_b):
    batch, seq, d = x.shape
    depth = w_ada.shape[0]
    sbw = sb_norm_w.shape[1]
    sb_heads = sbw // SB_HEAD_DIM
    ssd_heads = dt_bias.shape[1]
    n_main = w_in.shape[2] - ssd_heads
    alpha = (2 * depth) ** 0.25
    tb = 256 if (batch * seq * TOP_K) // w_gate.shape[1] >= 1024 else 128

    mod_all = _ada_call(c, w_ada, b_ada).reshape(depth, batch, 6, d)
    x2d = x.reshape(batch * seq, d)
    for l in range(depth):
        mod = mod_all[l]
        w_main = w_in[l, :, :n_main].astype(BF16)
        w_dt = jnp.pad(w_in[l, :, n_main:], ((0, 0), (0, 128 - ssd_heads))).astype(BF16)
        proj, dtp = _inproj_call(x2d, mod, w_main, w_dt, seq)
        o_sb = _sb_call(proj, sb_norm_w[l], batch, seq, sb_heads)
        o_ssd = _ssd_call(proj, dtp[:, :ssd_heads], conv_w[l], conv_b[l], dt_bias[l], a_log[l], d_skip[l],
                          ssd_norm_w[l], batch, seq, sbw)
        x1, h2 = _outproj_call(o_sb, o_ssd, x2d, mod, w_out[l].astype(BF16), ln1_g[l], ln1_b[l], seq, alpha)
        x2d = _moe(h2, x1, mod, w_router[l].T.astype(BF16), router_bias[l], w_gate, w_up, w_down, l,
                   ws_gate[l].astype(BF16), ws_up[l].astype(BF16), ws_down[l].astype(BF16), ln2_g[l], ln2_b[l],
                   seq, alpha, tb)
    return x2d.reshape(batch, seq, d)
```

```python
import functools
import math

import jax
import jax.numpy as jnp
from jax import lax
from jax.experimental import pallas as pl
from jax.experimental.pallas import tpu as pltpu

F32 = jnp.float32
BF16 = jnp.bfloat16

SB_HEAD_DIM = 128
SSD_HEAD_DIM = 64
SSD_GROUPS = 2
SSD_STATE = 128
SSD_CHUNK = 128
TOP_K = 8
N_EXPERT_GROUPS = 8
TOPK_GROUPS = 4
ROUTED_SCALE = 2.5
LN_EPS = 1e-5
RMS_EPS = 1e-6
SB_DEAD_LOG = -110.0

VMEM_LIMIT_BYTES = 56 * 1024 * 1024


def _params(sem, vmem=VMEM_LIMIT_BYTES):
    return pltpu.CompilerParams(dimension_semantics=sem, vmem_limit_bytes=vmem)


def _pick(n, cands):
    for c in cands:
        if n % c == 0:
            return c
    raise ValueError(f"no tile in {cands} divides {n}")


def _softplus(z):
    return jnp.maximum(z, 0.0) + jnp.log(1.0 + jnp.exp(-jnp.abs(z)))


def _silu(v):
    return v * jax.nn.sigmoid(v)


def _split3(v):
    hi = v.astype(BF16)
    r1 = v - hi.astype(F32)
    mid = r1.astype(BF16)
    lo = (r1 - mid.astype(F32)).astype(BF16)
    return hi, mid, lo


def _dot(a, b):
    return jnp.dot(a, b, preferred_element_type=F32)


def _dot_nt(a, b):
    return lax.dot_general(a, b, (((1,), (1,)), ((), ())), preferred_element_type=F32)


def _dot_tn(a, b):
    return lax.dot_general(a, b, (((0,), (0,)), ((), ())), preferred_element_type=F32)


def _pack_halves(v):
    half = v.shape[1] // 2
    hi = lax.bitcast_convert_type(v[:, :half].astype(BF16).astype(F32), jnp.uint32)
    lo = lax.bitcast_convert_type(v[:, half:].astype(BF16).astype(F32), jnp.uint32)
    return hi | (lo >> 16)


def _unpack_halves(w):
    hi = lax.bitcast_convert_type(w & jnp.uint32(0xFFFF0000), F32)
    lo = lax.bitcast_convert_type(w << 16, F32)
    return hi, lo


def _ln_stats(v):
    mu = jnp.mean(v, axis=-1, keepdims=True)
    d = v - mu
    var = jnp.mean(d * d, axis=-1, keepdims=True)
    return d * lax.rsqrt(var + LN_EPS)


def _ada_kernel(c_ref, w_ref, b_ref, o_ref):
    ca = _silu(c_ref[...]).astype(BF16)
    o_ref[0] = _dot(ca, w_ref[0].astype(BF16)) + b_ref[0]


def _ada_call(c, w_ada, b_ada):
    depth, d, n = w_ada.shape
    b = c.shape[0]
    tn = _pick(n, (512, 256, 128))
    return pl.pallas_call(
        _ada_kernel,
        out_shape=jax.ShapeDtypeStruct((depth, b, n), F32),
        grid=(depth, n // tn),
        in_specs=[
            pl.BlockSpec((b, d), lambda l, j: (0, 0)),
            pl.BlockSpec((1, d, tn), lambda l, j: (l, 0, j)),
            pl.BlockSpec((1, 1, tn), lambda l, j: (l, 0, j)),
        ],
        out_specs=pl.BlockSpec((1, b, tn), lambda l, j: (l, 0, j)),
        compiler_params=_params(("arbitrary", "arbitrary")),
        name="ada_mod",
    )(c, w_ada, b_ada.reshape(depth, 1, n))


def _inproj_kernel(x_ref, mod_ref, w_ref, wdt_ref, o_ref, dt_ref, h_scr):
    @pl.when(pl.program_id(1) == 0)
    def _():
        xn = _ln_stats(x_ref[...])
        h = xn * (1.0 + mod_ref[0, 1:2, :]) + mod_ref[0, 0:1, :]
        hb = h.astype(BF16)
        h_scr[...] = hb
        dt_ref[...] = _dot(hb, wdt_ref[...])

    o_ref[...] = _dot(h_scr[...], w_ref[...]).astype(BF16)


def _inproj_call(x2d, mod, w_main, w_dt, seq):
    t, d = x2d.shape
    n = w_main.shape[1]
    tm = _pick(seq, (1024, 512, 256, 128))
    tn = _pick(n, (512, 256, 128))
    per_b = seq // tm
    return pl.pallas_call(
        _inproj_kernel,
        out_shape=(jax.ShapeDtypeStruct((t, n), BF16), jax.ShapeDtypeStruct((t, 128), F32)),
        grid=(t // tm, n // tn),
        in_specs=[
            pl.BlockSpec((tm, d), lambda i, j: (i, 0)),
            pl.BlockSpec((1, 6, d), lambda i, j: (i // per_b, 0, 0)),
            pl.BlockSpec((d, tn), lambda i, j: (0, j)),
            pl.BlockSpec((d, 128), lambda i, j: (0, 0)),
        ],
        out_specs=(
            pl.BlockSpec((tm, tn), lambda i, j: (i, j)),
            pl.BlockSpec((tm, 128), lambda i, j: (i, 0)),
        ),
        scratch_shapes=[pltpu.VMEM((tm, d), BF16)],
        compiler_params=_params(("arbitrary", "arbitrary")),
        name="in_proj",
    )(x2d, mod, w_main, w_dt)


def _sb_kernel(q_ref, k_ref, v_ref, nw_ref, o_ref, *, tq, tk, hg):
    qi = pl.program_id(2)
    hd = SB_HEAD_DIM
    scale = 1.0 / math.sqrt(hd)
    nd = tq // tk
    rj = lax.broadcasted_iota(jnp.int32, (tq, tk), 0)
    cs = lax.broadcasted_iota(jnp.int32, (tq, tk), 1)
    lr = lax.broadcasted_iota(jnp.int32, (tk, tk), 0)
    lc = lax.broadcasted_iota(jnp.int32, (tk, tk), 1)
    later = jnp.where(lr > lc, 1.0, 0.0).astype(BF16)
    qs = [q_ref[:, g * hd:(g + 1) * hd] for g in range(hg)]

    def block(kb, accs, runs, offset):
        start = pl.multiple_of(kb * tk, tk)
        new_accs, new_runs = [], []
        for g in range(hg):
            k = k_ref[pl.ds(start, tk), g * hd:(g + 1) * hd]
            v = v_ref[pl.ds(start, tk), g * hd:(g + 1) * hd]
            z = _dot_nt(qs[g], k) * scale
            sp = _softplus(z)
            if offset is None:
                lom = -sp
            else:
                mask = (cs + offset) < rj
                lom = jnp.where(mask, -sp, 0.0)
            hi = lom.astype(BF16)
            lo = (lom - hi.astype(F32)).astype(BF16)
            between = _dot(hi, later) + _dot(lo, later) + runs[g]
            w = jnp.exp(z - sp + between)
            if offset is not None:
                w = jnp.where(mask, w, 0.0)
            new_accs.append(accs[g] + _dot(w.astype(BF16), v))
            new_runs.append(runs[g] + jnp.sum(lom, axis=1, keepdims=True))
        return new_accs, new_runs

    accs = [jnp.zeros((tq, hd), F32) for _ in range(hg)]
    runs = [jnp.zeros((tq, 1), F32) for _ in range(hg)]
    for j in range(nd):
        accs, runs = block(qi * nd + (nd - 1 - j), accs, runs, (nd - 1 - j) * tk)

    def run_max(rs):
        m = rs[0]
        for r in rs[1:]:
            m = jnp.maximum(m, r)
        return jnp.max(m)

    def cond(carry):
        i, top, _, _ = carry
        return jnp.logical_and(i < qi * nd, top > SB_DEAD_LOG)

    def body(carry):
        i, _, accs, runs = carry
        accs, runs = block(qi * nd - 1 - i, list(accs), list(runs), None)
        return i + 1, run_max(runs), tuple(accs), tuple(runs)

    _, _, accs, _ = lax.while_loop(cond, body, (jnp.int32(0), run_max(runs), tuple(accs), tuple(runs)))
    for g in range(hg):
        acc = accs[g]
        o = acc * lax.rsqrt(jnp.mean(acc * acc, axis=-1, keepdims=True) + RMS_EPS) * nw_ref[:, g * hd:(g + 1) * hd]
        o_ref[:, g * hd:(g + 1) * hd] = o.astype(BF16)


def _sb_call(proj, sb_norm_w, batch, seq, heads):
    t = proj.shape[0]
    tq = _pick(seq, (256, 128))
    tk = tq
    hg = 4 if heads % 4 == 0 else 2
    nq = seq // tq
    hb = heads // hg
    wd = hg * SB_HEAD_DIM
    return pl.pallas_call(
        functools.partial(_sb_kernel, tq=tq, tk=tk, hg=hg),
        out_shape=jax.ShapeDtypeStruct((t, heads * SB_HEAD_DIM), BF16),
        grid=(batch, hb, nq),
        in_specs=[
            pl.BlockSpec((tq, wd), lambda b, h, i: (b * nq + i, h)),
            pl.BlockSpec((seq, wd), lambda b, h, i: (b, hb + h)),
            pl.BlockSpec((seq, wd), lambda b, h, i: (b, 2 * hb + h)),
            pl.BlockSpec((1, wd), lambda b, h, i: (0, h)),
        ],
        out_specs=pl.BlockSpec((tq, wd), lambda b, h, i: (b * nq + i, h)),
        compiler_params=_params(("arbitrary", "arbitrary", "arbitrary")),
        name="sb_attn",
    )(proj, proj, proj, sb_norm_w.reshape(1, -1))


def _ssd_kernel(z_ref, xs_ref, b_ref, c_ref, dt_ref, dtt_ref, cw_ref, cb_ref, dtb_ref, dtbt_ref,
                alog_ref, alogt_ref, dskip_ref, nw_ref, o_ref, prev_scr, state_scr, *, heads):
    ln = xs_ref.shape[0]
    w = xs_ref.shape[1]
    gn = b_ref.shape[1]
    n = gn // SSD_GROUPS
    wg = w // SSD_GROUPS
    hpg = heads // SSD_GROUPS

    @pl.when(pl.program_id(1) == 0)
    def _():
        prev_scr[...] = jnp.zeros_like(prev_scr)
        state_scr[...] = jnp.zeros_like(state_scr)

    raw = jnp.concatenate([xs_ref[...], b_ref[...], c_ref[...]], axis=1).astype(F32)
    pr = prev_scr[...]
    ch = raw.shape[1]
    row = lax.broadcasted_iota(jnp.int32, (ln, ch), 0)
    taps = cw_ref.shape[0]
    acc = raw * cw_ref[taps - 1:taps, :] + cb_ref[...]
    for sh in range(1, taps):
        shifted = jnp.where(row < sh, pltpu.roll(pr, sh, 0), pltpu.roll(raw, sh, 0))
        acc = acc + shifted * cw_ref[taps - 1 - sh:taps - sh, :]
    prev_scr[...] = raw
    xbc = _silu(acc)
    xs = xbc[:, :w]
    bm = xbc[:, w:w + gn]
    cm = xbc[:, w + gn:]

    dtv = _softplus(dt_ref[0] + dtb_ref[...])
    dtvt = _softplus(dtt_ref[0] + dtbt_ref[...])
    a_full = -jnp.exp(alog_ref[...])
    a_t = -jnp.exp(alogt_ref[...])

    eh = lax.broadcasted_iota(jnp.int32, (heads, w), 0)
    ec = lax.broadcasted_iota(jnp.int32, (heads, w), 1)
    expand = jnp.where(ec // SSD_HEAD_DIM == eh, 1.0, 0.0).astype(BF16)
    d_hi, d_mid, d_lo = _split3(dtv)
    dt_exp = _dot(d_hi, expand) + _dot(d_mid, expand) + _dot(d_lo, expand)

    rl = lax.broadcasted_iota(jnp.int32, (ln, ln), 0)
    cl = lax.broadcasted_iota(jnp.int32, (ln, ln), 1)
    tril = cl <= rl
    incl = jnp.where(tril, 1.0, 0.0).astype(BF16)
    incl_t = jnp.where(rl <= cl, 1.0, 0.0).astype(BF16)

    a_exp = dt_exp * a_full
    a_hi, a_mid, a_lo = _split3(a_exp)
    acum = _dot(incl, a_hi) + _dot(incl, a_mid) + _dot(incl, a_lo)
    at = dtvt * a_t
    t_hi, t_mid, t_lo = _split3(at)
    acum_t = _dot(t_hi, incl_t) + _dot(t_mid, incl_t) + _dot(t_lo, incl_t)

    xdt = xs * dt_exp
    last = acum[ln - 1:ln, :]
    xdd = (xdt * jnp.exp(last - acum)).astype(BF16)
    xdtb = xdt.astype(BF16)
    ea = jnp.exp(acum)
    cdec = jnp.exp(last)

    col_head = lax.broadcasted_iota(jnp.int32, (ln, wg), 1) // SSD_HEAD_DIM
    ys = []
    for g in range(SSD_GROUPS):
        bg = bm[:, g * n:(g + 1) * n].astype(BF16)
        cg = cm[:, g * n:(g + 1) * n].astype(BF16)
        cb = _dot_nt(cg, bg)
        ms = []
        for e in range(hpg):
            h = g * hpg + e
            ac = acum[:, h * SSD_HEAD_DIM:h * SSD_HEAD_DIM + 1]
            ar = acum_t[h:h + 1, :]
            lmat = jnp.exp(jnp.where(tril, ac - ar, -jnp.inf))
            ms.append((cb * lmat).astype(BF16))
        mcat = jnp.concatenate(ms, axis=1)
        xg = xdtb[:, g * wg:(g + 1) * wg]
        xbd = jnp.concatenate([jnp.where(col_head == e, xg, jnp.zeros_like(xg)) for e in range(hpg)], axis=0)
        y_diag = _dot(mcat, xbd)
        sg = state_scr[g]
        y_off = _dot(cg, sg.astype(BF16)) * ea[:, g * wg:(g + 1) * wg]
        s_new = _dot_tn(bg, xdd[:, g * wg:(g + 1) * wg])
        state_scr[g] = sg * cdec[:, g * wg:(g + 1) * wg] + s_new
        ys.append(y_diag + y_off)
    y = jnp.concatenate(ys, axis=1) + xs * dskip_ref[...]
    y = y * _silu(z_ref[...].astype(F32))
    o = y * lax.rsqrt(jnp.mean(y * y, axis=-1, keepdims=True) + RMS_EPS) * nw_ref[...]
    o_ref[...] = o.astype(BF16)


def _ssd_call(proj, dt, conv_w, conv_b, dt_bias, a_log, d_skip, ssd_norm_w, batch, seq, sbw):
    t = proj.shape[0]
    heads = dt_bias.shape[0]
    w = heads * SSD_HEAD_DIM
    gn = SSD_GROUPS * SSD_STATE
    ch = w + 2 * gn
    ln = SSD_CHUNK
    nc = seq // ln
    dt3 = dt.reshape(batch, seq, heads)
    dtt = jnp.swapaxes(dt3, 1, 2)
    rep = lambda v: jnp.repeat(v, SSD_HEAD_DIM).reshape(1, w)
    z_blk, xs_blk = 3 * sbw // w, (3 * sbw + w) // w
    b_blk, c_blk = (3 * sbw + 2 * w) // gn, (3 * sbw + 2 * w + gn) // gn
    assert 3 * sbw % w == 0 and (3 * sbw + 2 * w) % gn == 0
    full = lambda shape: pl.BlockSpec(shape, lambda b, c: (0,) * len(shape))
    return pl.pallas_call(
        functools.partial(_ssd_kernel, heads=heads),
        out_shape=jax.ShapeDtypeStruct((t, w), BF16),
        grid=(batch, nc),
        in_specs=[
            pl.BlockSpec((ln, w), lambda b, c: (b * nc + c, z_blk)),
            pl.BlockSpec((ln, w), lambda b, c: (b * nc + c, xs_blk)),
            pl.BlockSpec((ln, gn), lambda b, c: (b * nc + c, b_blk)),
            pl.BlockSpec((ln, gn), lambda b, c: (b * nc + c, c_blk)),
            pl.BlockSpec((1, ln, heads), lambda b, c: (b, c, 0)),
            pl.BlockSpec((1, heads, ln), lambda b, c: (b, 0, c)),
            full(conv_w.shape), full((1, ch)), full((1, heads)), full((heads, 1)),
            full((1, w)), full((heads, 1)), full((1, w)), full((1, w)),
        ],
        out_specs=pl.BlockSpec((ln, w), lambda b, c: (b * nc + c, 0)),
        scratch_shapes=[pltpu.VMEM((ln, ch), F32), pltpu.VMEM((SSD_GROUPS, SSD_STATE, w // SSD_GROUPS), F32)],
        compiler_params=_params(("arbitrary", "arbitrary")),
        name="ssd",
    )(proj, proj, proj, proj, dt3, dtt, conv_w, conv_b.reshape(1, ch), dt_bias.reshape(1, heads),
      dt_bias.reshape(heads, 1), rep(a_log), a_log.reshape(heads, 1), rep(d_skip), ssd_norm_w.reshape(1, w))


def _outproj_kernel(osb_ref, ossd_ref, x_ref, mod_ref, w1_ref, w2_ref, g_ref, b_ref, x1_ref, h2_ref, *, alpha):
    m = _dot(osb_ref[...], w1_ref[...]) + _dot(ossd_ref[...], w2_ref[...])
    v = alpha * x_ref[...] + (1.0 + mod_ref[0, 2:3, :]) * m
    x1 = _ln_stats(v) * g_ref[...] + b_ref[...]
    x1_ref[...] = x1
    h2_ref[...] = _pack_halves(_ln_stats(x1) * (1.0 + mod_ref[0, 4:5, :]) + mod_ref[0, 3:4, :])


def _outproj_call(o_sb, o_ssd, x2d, mod, w_out_b, ln_g, ln_b, seq, alpha):
    t, d = x2d.shape
    sbw, ssw = o_sb.shape[1], o_ssd.shape[1]
    tm = _pick(seq, (256, 128))
    per_b = seq // tm
    return pl.pallas_call(
        functools.partial(_outproj_kernel, alpha=alpha),
        out_shape=(jax.ShapeDtypeStruct((t, d), F32), jax.ShapeDtypeStruct((t, d // 2), jnp.uint32)),
        grid=(t // tm,),
        in_specs=[
            pl.BlockSpec((tm, sbw), lambda i: (i, 0)),
            pl.BlockSpec((tm, ssw), lambda i: (i, 0)),
            pl.BlockSpec((tm, d), lambda i: (i, 0)),
            pl.BlockSpec((1, 6, d), lambda i: (i // per_b, 0, 0)),
            pl.BlockSpec((sbw, d), lambda i: (0, 0)),
            pl.BlockSpec((ssw, d), lambda i: (0, 0)),
            pl.BlockSpec((1, d), lambda i: (0, 0)),
            pl.BlockSpec((1, d), lambda i: (0, 0)),
        ],
        out_specs=(pl.BlockSpec((tm, d), lambda i: (i, 0)), pl.BlockSpec((tm, d // 2), lambda i: (i, 0))),
        compiler_params=_params(("arbitrary",)),
        name="out_proj",
    )(o_sb, o_ssd, x2d, mod, w_out_b[:sbw], w_out_b[sbw:], ln_g.reshape(1, d), ln_b.reshape(1, d))


def _router_kernel(h_ref, wr_ref, bias_ref, idx_ref, rnk_ref, gate_ref, cnt_ref, carry_scr):
    ne = wr_ref.shape[0]
    tm = h_ref.shape[0]
    per_g = ne // N_EXPERT_GROUPS
    neg = -jnp.inf

    @pl.when(pl.program_id(0) == 0)
    def _():
        carry_scr[...] = jnp.zeros_like(carry_scr)

    ha, hb = _unpack_halves(h_ref[...])
    half = ha.shape[1]
    logits = _dot_nt(wr_ref[:, :half], ha.astype(BF16)) + _dot_nt(wr_ref[:, half:], hb.astype(BF16))
    scores = jax.nn.sigmoid(logits)
    biased = scores + bias_ref[...]
    s3 = scores.reshape(N_EXPERT_GROUPS, per_g, tm)
    b3 = biased.reshape(N_EXPERT_GROUPS, per_g, tm)
    shape3 = (N_EXPERT_GROUPS, per_g, tm)
    j_io = lax.broadcasted_iota(jnp.int32, shape3, 1)
    g_io = lax.broadcasted_iota(jnp.int32, shape3, 0)
    e_io = g_io * per_g + j_io

    m1 = jnp.max(b3, axis=1, keepdims=True)
    i1 = jnp.min(jnp.where(b3 == m1, j_io, per_g), axis=1, keepdims=True)
    m2 = jnp.max(jnp.where(j_io == i1, neg, b3), axis=1, keepdims=True)
    gs = m1 + m2
    gg = lax.broadcasted_iota(jnp.int32, gs.shape, 0)
    gmask = jnp.zeros(gs.shape, F32)
    cur = gs
    for _ in range(TOPK_GROUPS):
        m = jnp.max(cur, axis=0, keepdims=True)
        gi = jnp.min(jnp.where(cur == m, gg, N_EXPERT_GROUPS), axis=0, keepdims=True)
        pick = gg == gi
        gmask = jnp.where(pick, 1.0, gmask)
        cur = jnp.where(pick, neg, cur)
    cur = jnp.where(jnp.broadcast_to(gmask, shape3) > 0.5, b3, neg)

    sel = jnp.zeros(shape3, F32)
    picks = []
    for _ in range(TOP_K):
        m = jnp.max(jnp.max(cur, axis=1, keepdims=True), axis=0, keepdims=True)
        ei = jnp.min(jnp.min(jnp.where(cur == m, e_io, ne), axis=1, keepdims=True), axis=0, keepdims=True)
        pick = e_io == ei
        sel = jnp.where(pick, 1.0, sel)
        cur = jnp.where(pick, neg, cur)
        picks.append((pick, ei))

    selw = sel * s3
    denom = jnp.sum(jnp.sum(selw, axis=1, keepdims=True), axis=0, keepdims=True)
    gates3 = selw / denom * ROUTED_SCALE

    sel2 = sel.reshape(ne, tm)
    rj = lax.broadcasted_iota(jnp.int32, (tm, tm), 0)
    cs = lax.broadcasted_iota(jnp.int32, (tm, tm), 1)
    before = jnp.where(rj < cs, 1.0, 0.0).astype(BF16)
    carry = carry_scr[...]
    rank2 = _dot(sel2.astype(BF16), before) + carry[:, 0:1]
    rank3 = rank2.reshape(shape3)
    new_carry = carry + jnp.sum(sel2, axis=1, keepdims=True)
    carry_scr[...] = new_carry
    cnt_ref[...] = new_carry.astype(jnp.int32)

    for k, (pick, ei) in enumerate(picks):
        red = lambda v: jnp.sum(jnp.sum(v, axis=1, keepdims=True), axis=0, keepdims=True).reshape(1, tm)
        idx_ref[k:k + 1, :] = ei.reshape(1, tm)
        rnk_ref[k:k + 1, :] = red(jnp.where(pick, rank3, 0.0)).astype(jnp.int32)
        gate_ref[k:k + 1, :] = red(jnp.where(pick, gates3, 0.0))


def _router_call(h2p, wr_t, bias):
    t = h2p.shape[0]
    ne, d = wr_t.shape
    tm = _pick(t, (512, 256))
    if t == tm:
        tm = tm // 2
    return pl.pallas_call(
        _router_kernel,
        out_shape=(jax.ShapeDtypeStruct((TOP_K, t), jnp.int32), jax.ShapeDtypeStruct((TOP_K, t), jnp.int32),
                   jax.ShapeDtypeStruct((TOP_K, t), F32), jax.ShapeDtypeStruct((ne, 128), jnp.int32)),
        grid=(t // tm,),
        in_specs=[
            pl.BlockSpec((tm, d // 2), lambda i: (i, 0)),
            pl.BlockSpec((ne, d), lambda i: (0, 0)),
            pl.BlockSpec((ne, 1), lambda i: (0, 0)),
        ],
        out_specs=(pl.BlockSpec((TOP_K, tm), lambda i: (0, i)), pl.BlockSpec((TOP_K, tm), lambda i: (0, i)),
                   pl.BlockSpec((TOP_K, tm), lambda i: (0, i)), pl.BlockSpec((ne, 128), lambda i: (0, 0))),
        scratch_shapes=[pltpu.VMEM((ne, 128), F32)],
        compiler_params=_params(("arbitrary",)),
        name="router",
    )(h2p, wr_t, bias.reshape(ne, 1))


def _dispatch_kernel(ps_ref, pc_ref, nu_ref, dest_ref, h_ref, xs_hbm, zbuf, idx_smem, isem, ssem, zsem, tsem):
    tm = h_ref.shape[0]
    ne = ps_ref.shape[0]
    tb = zbuf.shape[0]
    nblk = xs_hbm.shape[0] // tb

    @pl.when(pl.program_id(0) == 0)
    def _():
        zbuf[...] = jnp.zeros_like(zbuf)

        def tail(r, c):
            pltpu.make_async_copy(zbuf, xs_hbm.at[pl.ds(pl.multiple_of(r * tb, tb), tb)], tsem).start()
            return c

        def tail_drain(r, c):
            pltpu.make_async_copy(zbuf, xs_hbm.at[pl.ds(0, tb)], tsem).wait()
            return c

        lax.fori_loop(nu_ref[0], nblk, tail, 0)
        lax.fori_loop(nu_ref[0], nblk, tail_drain, 0)

        def fill(e, c):
            def one(j, c2):
                pltpu.make_async_copy(zbuf.at[0], xs_hbm.at[ps_ref[e] + j], zsem).start()
                return c2
            return lax.fori_loop(0, pc_ref[e], one, c)

        def drain(e, c):
            def one(j, c2):
                pltpu.make_async_copy(zbuf.at[0], xs_hbm.at[0], zsem).wait()
                return c2
            return lax.fori_loop(0, pc_ref[e], one, c)

        lax.fori_loop(0, ne, fill, 0)
        lax.fori_loop(0, ne, drain, 0)

    cp = pltpu.make_async_copy(dest_ref.at[0, 0], idx_smem, isem)
    cp.start()
    cp.wait()

    def issue(i, c):
        for k in range(TOP_K):
            pltpu.make_async_copy(h_ref.at[i], xs_hbm.at[idx_smem[k * tm + i]], ssem).start()
        return c

    lax.fori_loop(0, tm, issue, 0, unroll=2)
    pltpu.make_async_copy(xs_hbm.at[pl.ds(0, TOP_K * tm)], xs_hbm.at[pl.ds(0, TOP_K * tm)], ssem).wait()


def _dispatch_call(pad_start, pad_cnt, n_used, dest, h2p, n_rows, tb):
    t, d = h2p.shape
    tm = _pick(t, (512, 256))
    nt = t // tm
    dest_t = dest.reshape(TOP_K, nt, tm).transpose(1, 0, 2).reshape(nt, 1, TOP_K * tm)
    grid_spec = pltpu.PrefetchScalarGridSpec(
        num_scalar_prefetch=3,
        grid=(nt,),
        in_specs=[
            pl.BlockSpec((1, 1, TOP_K * tm), lambda i, ps, pc, nu: (i, 0, 0)),
            pl.BlockSpec((tm, d), lambda i, ps, pc, nu: (i, 0)),
        ],
        out_specs=pl.BlockSpec(memory_space=pl.ANY),
        scratch_shapes=[pltpu.VMEM((tb, d), h2p.dtype), pltpu.SMEM((TOP_K * tm,), jnp.int32), pltpu.SemaphoreType.DMA,
                        pltpu.SemaphoreType.DMA, pltpu.SemaphoreType.DMA, pltpu.SemaphoreType.DMA],
    )
    return pl.pallas_call(
        _dispatch_kernel,
        out_shape=jax.ShapeDtypeStruct((n_rows, d), h2p.dtype),
        grid_spec=grid_spec,
        compiler_params=_params(("arbitrary",)),
        name="dispatch",
    )(pad_start, pad_cnt, n_used, dest_t, h2p)


def _expert_kernel(be_ref, nu_ref, x_ref, wg_ref, wu_ref, wd_ref, o_ref, wgb, wub, wdb):
    r = pl.program_id(0)

    @pl.when(r < nu_ref[0])
    def _():
        @pl.when(jnp.logical_or(r == 0, be_ref[r] != be_ref[jnp.maximum(r - 1, 0)]))
        def _():
            wgb[...] = wg_ref[0, 0].astype(BF16)
            wub[...] = wu_ref[0, 0].astype(BF16)
            wdb[...] = wd_ref[0, 0].astype(BF16)

        xa, xb = _unpack_halves(x_ref[...])
        xa, xb = xa.astype(BF16), xb.astype(BF16)
        half = xa.shape[1]
        g = _dot(xa, wgb[:half, :]) + _dot(xb, wgb[half:, :])
        u = _dot(xa, wub[:half, :]) + _dot(xb, wub[half:, :])
        o_ref[...] = _pack_halves(_dot((_silu(g) * u).astype(BF16), wdb[...]))

    @pl.when(r >= nu_ref[0])
    def _():
        o_ref[...] = jnp.zeros_like(o_ref)


def _expert_call(block_expert, n_used, xs, w_gate, w_up, w_down, layer, tb):
    n_rows = xs.shape[0]
    d, de = w_gate.shape[2:]
    nblk = n_rows // tb
    blk = lambda r, be, nu: jnp.minimum(r, nu[0] - 1)
    wmap = lambda r, be, nu: (layer, be[blk(r, be, nu)], 0, 0)
    grid_spec = pltpu.PrefetchScalarGridSpec(
        num_scalar_prefetch=2,
        grid=(nblk,),
        in_specs=[
            pl.BlockSpec((tb, d // 2), lambda r, be, nu: (blk(r, be, nu), 0)),
            pl.BlockSpec((1, 1, d, de), wmap),
            pl.BlockSpec((1, 1, d, de), wmap),
            pl.BlockSpec((1, 1, de, d), wmap),
        ],
        out_specs=pl.BlockSpec((tb, d // 2), lambda r, be, nu: (r, 0)),
        scratch_shapes=[pltpu.VMEM((d, de), BF16), pltpu.VMEM((d, de), BF16), pltpu.VMEM((de, d), BF16)],
    )
    return pl.pallas_call(
        _expert_kernel,
        out_shape=jax.ShapeDtypeStruct((n_rows, d // 2), jnp.uint32),
        grid_spec=grid_spec,
        compiler_params=_params(("arbitrary",)),
        name="experts",
    )(block_expert, n_used, xs, w_gate, w_up, w_down)


def _combine_kernel(pos_ref, posn_ref, gate_ref, h_ref, x_ref, mod_ref, o_hbm, sg_ref, su_ref, sd_ref, g_ref, b_ref,
                    out_ref, gbuf, idx_smem, isem, gsem, *, alpha):
    i = pl.program_id(0)
    tm = h_ref.shape[0]
    n = TOP_K * tm

    def gather(src_ref, slot):
        cp = pltpu.make_async_copy(src_ref.at[0, 0], idx_smem, isem)
        cp.start()
        cp.wait()

        def issue(j, c):
            pltpu.make_async_copy(o_hbm.at[idx_smem[j]], gbuf.at[slot, j], gsem.at[slot]).start()
            return c

        lax.fori_loop(0, n, issue, 0, unroll=8)

    @pl.when(i == 0)
    def _():
        gather(pos_ref, 0)

    @pl.when(i + 1 < pl.num_programs(0))
    def _():
        gather(posn_ref, (i + 1) % 2)

    ha, hb = _unpack_halves(h_ref[...])
    ha, hb = ha.astype(BF16), hb.astype(BF16)
    half = ha.shape[1]
    sg = _dot(ha, sg_ref[:half, :]) + _dot(hb, sg_ref[half:, :])
    su = _dot(ha, su_ref[:half, :]) + _dot(hb, su_ref[half:, :])
    f = _dot((_silu(sg) * su).astype(BF16), sd_ref[...])
    fa, fb = f[:, :half], f[:, half:]
    slot = i % 2
    pltpu.make_async_copy(o_hbm.at[pl.ds(0, n)], gbuf.at[slot], gsem.at[slot]).wait()
    gate = gate_ref[...]
    for k in range(TOP_K):
        oa, ob = _unpack_halves(gbuf[slot, pl.ds(k * tm, tm), :])
        fa = fa + oa * gate[:, k:k + 1]
        fb = fb + ob * gate[:, k:k + 1]
    f = jnp.concatenate([fa, fb], axis=1)
    v = alpha * x_ref[...] + (1.0 + mod_ref[0, 5:6, :]) * f
    out_ref[...] = _ln_stats(v) * g_ref[...] + b_ref[...]


def _combine_call(pos, gate_t, h2p, x1, mod, o_sorted, sg, su, sd, ln_g, ln_b, seq, alpha):
    t, d = x1.shape
    ds = sg.shape[1]
    tm = pos.shape[2] // TOP_K
    per_b = seq // tm
    nt = t // tm
    return pl.pallas_call(
        functools.partial(_combine_kernel, alpha=alpha),
        out_shape=jax.ShapeDtypeStruct((t, d), F32),
        grid=(nt,),
        in_specs=[
            pl.BlockSpec((1, 1, TOP_K * tm), lambda i: (i, 0, 0)),
            pl.BlockSpec((1, 1, TOP_K * tm), lambda i: (jnp.minimum(i + 1, nt - 1), 0, 0)),
            pl.BlockSpec((tm, TOP_K), lambda i: (i, 0)),
            pl.BlockSpec((tm, d // 2), lambda i: (i, 0)),
            pl.BlockSpec((tm, d), lambda i: (i, 0)),
            pl.BlockSpec((1, 6, d), lambda i: (i // per_b, 0, 0)),
            pl.BlockSpec(memory_space=pl.ANY),
            pl.BlockSpec((d, ds), lambda i: (0, 0)),
            pl.BlockSpec((d, ds), lambda i: (0, 0)),
            pl.BlockSpec((ds, d), lambda i: (0, 0)),
            pl.BlockSpec((1, d), lambda i: (0, 0)),
            pl.BlockSpec((1, d), lambda i: (0, 0)),
        ],
        out_specs=pl.BlockSpec((tm, d), lambda i: (i, 0)),
        scratch_shapes=[pltpu.VMEM((2, TOP_K * tm, d // 2), jnp.uint32), pltpu.SMEM((TOP_K * tm,), jnp.int32),
                        pltpu.SemaphoreType.DMA, pltpu.SemaphoreType.DMA((2,))],
        compiler_params=_params(("arbitrary",)),
        name="combine",
    )(pos, pos, gate_t, h2p, x1, mod, o_sorted, sg, su, sd, ln_g.reshape(1, d), ln_b.reshape(1, d))


def _moe(h2, x1, mod, wr_t, router_bias, w_gate, w_up, w_down, layer, sg, su, sd, ln_g, ln_b, seq, alpha, tb):
    t, d = x1.shape
    ne = w_gate.shape[1]
    idx, rnk, gate, cnt = _router_call(h2, wr_t, router_bias)
    counts = cnt[:, 0]
    padded = (counts + tb - 1) // tb * tb
    pend = jnp.cumsum(padded)
    pstart = pend - padded
    eids = jnp.arange(ne, dtype=jnp.int32)
    dest = rnk + jnp.sum(jnp.where(idx[..., None] == eids, pstart.astype(jnp.int32), 0), axis=-1)
    nblk = (t * TOP_K) // tb + ne
    blk_row = jnp.arange(nblk, dtype=jnp.int32) * tb
    block_expert = jnp.minimum(jnp.sum((pend[None, :] <= blk_row[:, None]).astype(jnp.int32), axis=1), ne - 1)
    n_used = (pend[-1] // tb).astype(jnp.int32).reshape(1)
    xs = _dispatch_call((pstart + counts).astype(jnp.int32), (padded - counts).astype(jnp.int32), n_used, dest, h2,
                        nblk * tb, tb)
    o_sorted = _expert_call(block_expert, n_used, xs, w_gate, w_up, w_down, layer, tb)
    tm = _pick(seq, (256, 128))
    pos = dest.reshape(TOP_K, t // tm, tm).transpose(1, 0, 2).reshape(t // tm, 1, TOP_K * tm)
    return _combine_call(pos, gate.T, h2, x1, mod, o_sorted, sg, su, sd, ln_g, ln_b, seq, alpha)


def kernel(x, c, w_ada, b_ada, w_in, conv_w, conv_b, dt_bias, a_log, d_skip, sb_norm_w, ssd_norm_w, w_out, ln1_g,
           ln1_b, w_router, router_bias, w_gate, w_up, w_down, ws_gate, ws_up, ws_down, ln2_g, ln2_b):
    batch, seq, d = x.shape
    depth = w_ada.shape[0]
    sbw = sb_norm_w.shape[1]
    sb_heads = sbw // SB_HEAD_DIM
    ssd_heads = dt_bias.shape[1]
    n_main = w_in.shape[2] - ssd_heads
    alpha = (2 * depth) ** 0.25
    tb = 256 if (batch * seq * TOP_K) // w_gate.shape[1] >= 1024 else 128

    mod_all = _ada_call(c, w_ada, b_ada).reshape(depth, batch, 6, d)
    x2d = x.reshape(batch * seq, d)
    for l in range(depth):
        mod = mod_all[l]
        w_main = w_in[l, :, :n_main].astype(BF16)
        w_dt = jnp.pad(w_in[l, :, n_main:], ((0, 0), (0, 128 - ssd_heads))).astype(BF16)
        proj, dtp = _inproj_call(x2d, mod, w_main, w_dt, seq)
        o_sb = _sb_call(proj, sb_norm_w[l], batch, seq, sb_heads)
        o_ssd = _ssd_call(proj, dtp[:, :ssd_heads], conv_w[l], conv_b[l], dt_bias[l], a_log[l], d_skip[l],
                          ssd_norm_w[l], batch, seq, sbw)
        x1, h2 = _outproj_call(o_sb, o_ssd, x2d, mod, w_out[l].astype(BF16), ln1_g[l], ln1_b[l], seq, alpha)
        x2d = _moe(h2, x1, mod, w_router[l].T.astype(BF16), router_bias[l], w_gate, w_up, w_down, l,
                   ws_gate[l].astype(BF16), ws_up[l].astype(BF16), ws_down[l].astype(BF16), ln2_g[l], ln2_b[l],
                   seq, alpha, tb)
    return x2d.reshape(batch, seq, d)
```

```python
import functools
import math

import jax
import jax.numpy as jnp
from jax import lax
from jax.experimental import pallas as pl
from jax.experimental.pallas import tpu as pltpu

F32 = jnp.float32
BF16 = jnp.bfloat16

SB_HEAD_DIM = 128
SSD_HEAD_DIM = 64
SSD_GROUPS = 2
SSD_STATE = 128
SSD_CHUNK = 128
TOP_K = 8
N_EXPERT_GROUPS = 8
TOPK_GROUPS = 4
ROUTED_SCALE = 2.5
LN_EPS = 1e-5
RMS_EPS = 1e-6
SB_DEAD_LOG = -110.0

VMEM_LIMIT_BYTES = 56 * 1024 * 1024


def _params(sem, vmem=VMEM_LIMIT_BYTES):
    return pltpu.CompilerParams(dimension_semantics=sem, vmem_limit_bytes=vmem)


def _pick(n, cands):
    for c in cands:
        if n % c == 0:
            return c
    raise ValueError(f"no tile in {cands} divides {n}")


def _softplus(z):
    return jnp.maximum(z, 0.0) + jnp.log(1.0 + jnp.exp(-jnp.abs(z)))


def _silu(v):
    return v * jax.nn.sigmoid(v)


def _split3(v):
    hi = v.astype(BF16)
    r1 = v - hi.astype(F32)
    mid = r1.astype(BF16)
    lo = (r1 - mid.astype(F32)).astype(BF16)
    return hi, mid, lo


def _dot(a, b):
    return jnp.dot(a, b, preferred_element_type=F32)


def _dot_nt(a, b):
    return lax.dot_general(a, b, (((1,), (1,)), ((), ())), preferred_element_type=F32)


def _dot_tn(a, b):
    return lax.dot_general(a, b, (((0,), (0,)), ((), ())), preferred_element_type=F32)


def _pack_halves(v):
    half = v.shape[1] // 2
    hi = lax.bitcast_convert_type(v[:, :half].astype(BF16).astype(F32), jnp.uint32)
    lo = lax.bitcast_convert_type(v[:, half:].astype(BF16).astype(F32), jnp.uint32)
    return hi | (lo >> 16)


def _unpack_halves(w):
    hi = lax.bitcast_convert_type(w & jnp.uint32(0xFFFF0000), F32)
    lo = lax.bitcast_convert_type(w << 16, F32)
    return hi, lo


def _ln_stats(v):
    mu = jnp.mean(v, axis=-1, keepdims=True)
    d = v - mu
    var = jnp.mean(d * d, axis=-1, keepdims=True)
    return d * lax.rsqrt(var + LN_EPS)


def _ada_kernel(c_ref, w_ref, b_ref, o_ref):
    ca = _silu(c_ref[...]).astype(BF16)
    o_ref[0] = _dot(ca, w_ref[0].astype(BF16)) + b_ref[0]


def _ada_call(c, w_ada, b_ada):
    depth, d, n = w_ada.shape
    b = c.shape[0]
    tn = _pick(n, (512, 256, 128))
    return pl.pallas_call(
        _ada_kernel,
        out_shape=jax.ShapeDtypeStruct((depth, b, n), F32),
        grid=(depth, n // tn),
        in_specs=[
            pl.BlockSpec((b, d), lambda l, j: (0, 0)),
            pl.BlockSpec((1, d, tn), lambda l, j: (l, 0, j)),
            pl.BlockSpec((1, 1, tn), lambda l, j: (l, 0, j)),
        ],
        out_specs=pl.BlockSpec((1, b, tn), lambda l, j: (l, 0, j)),
        compiler_params=_params(("arbitrary", "arbitrary")),
        name="ada_mod",
    )(c, w_ada, b_ada.reshape(depth, 1, n))


def _inproj_kernel(x_ref, mod_ref, w_ref, wdt_ref, o_ref, dt_ref, h_scr):
    @pl.when(pl.program_id(1) == 0)
    def _():
        xn = _ln_stats(x_ref[...])
        h = xn * (1.0 + mod_ref[0, 1:2, :]) + mod_ref[0, 0:1, :]
        hb = h.astype(BF16)
        h_scr[...] = hb
        dt_ref[...] = _dot(hb, wdt_ref[...])

    o_ref[...] = _dot(h_scr[...], w_ref[...]).astype(BF16)


def _inproj_call(x2d, mod, w_main, w_dt, seq):
    t, d = x2d.shape
    n = w_main.shape[1]
    tm = _pick(seq, (1024, 512, 256, 128))
    tn = _pick(n, (512, 256, 128))
    per_b = seq // tm
    return pl.pallas_call(
        _inproj_kernel,
        out_shape=(jax.ShapeDtypeStruct((t, n), BF16), jax.ShapeDtypeStruct((t, 128), F32)),
        grid=(t // tm, n // tn),
        in_specs=[
            pl.BlockSpec((tm, d), lambda i, j: (i, 0)),
            pl.BlockSpec((1, 6, d), lambda i, j: (i // per_b, 0, 0)),
            pl.BlockSpec((d, tn), lambda i, j: (0, j)),
            pl.BlockSpec((d, 128), lambda i, j: (0, 0)),
        ],
        out_specs=(
            pl.BlockSpec((tm, tn), lambda i, j: (i, j)),
            pl.BlockSpec((tm, 128), lambda i, j: (i, 0)),
        ),
        scratch_shapes=[pltpu.VMEM((tm, d), BF16)],
        compiler_params=_params(("arbitrary", "arbitrary")),
        name="in_proj",
    )(x2d, mod, w_main, w_dt)


def _sb_kernel(q_ref, k_ref, v_ref, nw_ref, o_ref, *, tq, tk, hg):
    qi = pl.program_id(2)
    hd = SB_HEAD_DIM
    scale = 1.0 / math.sqrt(hd)
    nd = tq // tk
    rj = lax.broadcasted_iota(jnp.int32, (tq, tk), 0)
    cs = lax.broadcasted_iota(jnp.int32, (tq, tk), 1)
    lr = lax.broadcasted_iota(jnp.int32, (tk, tk), 0)
    lc = lax.broadcasted_iota(jnp.int32, (tk, tk), 1)
    later = jnp.where(lr > lc, 1.0, 0.0).astype(BF16)
    qs = [q_ref[:, g * hd:(g + 1) * hd] for g in range(hg)]

    def block(kb, accs, runs, offset):
        start = pl.multiple_of(kb * tk, tk)
        new_accs, new_runs = [], []
        for g in range(hg):
            k = k_ref[pl.ds(start, tk), g * hd:(g + 1) * hd]
            v = v_ref[pl.ds(start, tk), g * hd:(g + 1) * hd]
            z = _dot_nt(qs[g], k) * scale
            sp = _softplus(z)
            if offset is None:
                lom = -sp
            else:
                mask = (cs + offset) < rj
                lom = jnp.where(mask, -sp, 0.0)
            hi = lom.astype(BF16)
            lo = (lom - hi.astype(F32)).astype(BF16)
            between = _dot(hi, later) + _dot(lo, later) + runs[g]
            w = jnp.exp(z - sp + between)
            if offset is not None:
                w = jnp.where(mask, w, 0.0)
            new_accs.append(accs[g] + _dot(w.astype(BF16), v))
            new_runs.append(runs[g] + jnp.sum(lom, axis=1, keepdims=True))
        return new_accs, new_runs

    accs = [jnp.zeros((tq, hd), F32) for _ in range(hg)]
    runs = [jnp.zeros((tq, 1), F32) for _ in range(hg)]
    for j in range(nd):
        accs, runs = block(qi * nd + (nd - 1 - j), accs, runs, (nd - 1 - j) * tk)

    def run_max(rs):
        m = rs[0]
        for r in rs[1:]:
            m = jnp.maximum(m, r)
        return jnp.max(m)

    def cond(carry):
        i, top, _, _ = carry
        return jnp.logical_and(i < qi * nd, top > SB_DEAD_LOG)

    def body(carry):
        i, _, accs, runs = carry
        accs, runs = block(qi * nd - 1 - i, list(accs), list(runs), None)
        return i + 1, run_max(runs), tuple(accs), tuple(runs)

    _, _, accs, _ = lax.while_loop(cond, body, (jnp.int32(0), run_max(runs), tuple(accs), tuple(runs)))
    for g in range(hg):
        acc = accs[g]
        o = acc * lax.rsqrt(jnp.mean(acc * acc, axis=-1, keepdims=True) + RMS_EPS) * nw_ref[:, g * hd:(g + 1) * hd]
        o_ref[:, g * hd:(g + 1) * hd] = o.astype(BF16)


def _sb_call(proj, sb_norm_w, batch, seq, heads):
    t = proj.shape[0]
    tq = _pick(seq, (256, 128))
    tk = tq
    hg = 4 if heads % 4 == 0 else 2
    nq = seq // tq
    hb = heads // hg
    wd = hg * SB_HEAD_DIM
    return pl.pallas_call(
        functools.partial(_sb_kernel, tq=tq, tk=tk, hg=hg),
        out_shape=jax.ShapeDtypeStruct((t, heads * SB_HEAD_DIM), BF16),
        grid=(batch, hb, nq),
        in_specs=[
            pl.BlockSpec((tq, wd), lambda b, h, i: (b * nq + i, h)),
            pl.BlockSpec((seq, wd), lambda b, h, i: (b, hb + h)),
            pl.BlockSpec((seq, wd), lambda b, h, i: (b, 2 * hb + h)),
            pl.BlockSpec((1, wd), lambda b, h, i: (0, h)),
        ],
        out_specs=pl.BlockSpec((tq, wd), lambda b, h, i: (b * nq + i, h)),
        compiler_params=_params(("arbitrary", "arbitrary", "arbitrary")),
        name="sb_attn",
    )(proj, proj, proj, sb_norm_w.reshape(1, -1))


def _ssd_kernel(z_ref, xs_ref, b_ref, c_ref, dt_ref, dtt_ref, cw_ref, cb_ref, dtb_ref, dtbt_ref,
                alog_ref, alogt_ref, dskip_ref, nw_ref, o_ref, prev_scr, state_scr, *, heads):
    ln = xs_ref.shape[0]
    w = xs_ref.shape[1]
    gn = b_ref.shape[1]
    n = gn // SSD_GROUPS
    wg = w // SSD_GROUPS
    hpg = heads // SSD_GROUPS

    @pl.when(pl.program_id(1) == 0)
    def _():
        prev_scr[...] = jnp.zeros_like(prev_scr)
        state_scr[...] = jnp.zeros_like(state_scr)

    raw = jnp.concatenate([xs_ref[...], b_ref[...], c_ref[...]], axis=1).astype(F32)
    pr = prev_scr[...]
    ch = raw.shape[1]
    row = lax.broadcasted_iota(jnp.int32, (ln, ch), 0)
    taps = cw_ref.shape[0]
    acc = raw * cw_ref[taps - 1:taps, :] + cb_ref[...]
    for sh in range(1, taps):
        shifted = jnp.where(row < sh, pltpu.roll(pr, sh, 0), pltpu.roll(raw, sh, 0))
        acc = acc + shifted * cw_ref[taps - 1 - sh:taps - sh, :]
    prev_scr[...] = raw
    xbc = _silu(acc)
    xs = xbc[:, :w]
    bm = xbc[:, w:w + gn]
    cm = xbc[:, w + gn:]

    dtv = _softplus(dt_ref[0] + dtb_ref[...])
    dtvt = _softplus(dtt_ref[0] + dtbt_ref[...])
    a_full = -jnp.exp(alog_ref[...])
    a_t = -jnp.exp(alogt_ref[...])

    eh = lax.broadcasted_iota(jnp.int32, (heads, w), 0)
    ec = lax.broadcasted_iota(jnp.int32, (heads, w), 1)
    expand = jnp.where(ec // SSD_HEAD_DIM == eh, 1.0, 0.0).astype(BF16)
    d_hi, d_mid, d_lo = _split3(dtv)
    dt_exp = _dot(d_hi, expand) + _dot(d_mid, expand) + _dot(d_lo, expand)

    rl = lax.broadcasted_iota(jnp.int32, (ln, ln), 0)
    cl = lax.broadcasted_iota(jnp.int32, (ln, ln), 1)
    tril = cl <= rl
    incl = jnp.where(tril, 1.0, 0.0).astype(BF16)
    incl_t = jnp.where(rl <= cl, 1.0, 0.0).astype(BF16)

    a_exp = dt_exp * a_full
    a_hi, a_mid, a_lo = _split3(a_exp)
    acum = _dot(incl, a_hi) + _dot(incl, a_mid) + _dot(incl, a_lo)
    at = dtvt * a_t
    t_hi, t_mid, t_lo = _split3(at)
    acum_t = _dot(t_hi, incl_t) + _dot(t_mid, incl_t) + _dot(t_lo, incl_t)

    xdt = xs * dt_exp
    last = acum[ln - 1:ln, :]
    xdd = (xdt * jnp.exp(last - acum)).astype(BF16)
    xdtb = xdt.astype(BF16)
    ea = jnp.exp(acum)
    cdec = jnp.exp(last)

    col_head = lax.broadcasted_iota(jnp.int32, (ln, wg), 1) // SSD_HEAD_DIM
    ys = []
    for g in range(SSD_GROUPS):
        bg = bm[:, g * n:(g + 1) * n].astype(BF16)
        cg = cm[:, g * n:(g + 1) * n].astype(BF16)
        cb = _dot_nt(cg, bg)
        ms = []
        for e in range(hpg):
            h = g * hpg + e
            ac = acum[:, h * SSD_HEAD_DIM:h * SSD_HEAD_DIM + 1]
            ar = acum_t[h:h + 1, :]
            lmat = jnp.exp(jnp.where(tril, ac - ar, -jnp.inf))
            ms.append((cb * lmat).astype(BF16))
        mcat = jnp.concatenate(ms, axis=1)
        xg = xdtb[:, g * wg:(g + 1) * wg]
        xbd = jnp.concatenate([jnp.where(col_head == e, xg, jnp.zeros_like(xg)) for e in range(hpg)], axis=0)
        y_diag = _dot(mcat, xbd)
        sg = state_scr[g]
        y_off = _dot(cg, sg.astype(BF16)) * ea[:, g * wg:(g + 1) * wg]
        s_new = _dot_tn(bg, xdd[:, g * wg:(g + 1) * wg])
        state_scr[g] = sg * cdec[:, g * wg:(g + 1) * wg] + s_new
        ys.append(y_diag + y_off)
    y = jnp.concatenate(ys, axis=1) + xs * dskip_ref[...]
    y = y * _silu(z_ref[...].astype(F32))
    o = y * lax.rsqrt(jnp.mean(y * y, axis=-1, keepdims=True) + RMS_EPS) * nw_ref[...]
    o_ref[...] = o.astype(BF16)


def _ssd_call(proj, dt, conv_w, conv_b, dt_bias, a_log, d_skip, ssd_norm_w, batch, seq, sbw):
    t = proj.shape[0]
    heads = dt_bias.shape[0]
    w = heads * SSD_HEAD_DIM
    gn = SSD_GROUPS * SSD_STATE
    ch = w + 2 * gn
    ln = SSD_CHUNK
    nc = seq // ln
    dt3 = dt.reshape(batch, seq, heads)
    dtt = jnp.swapaxes(dt3, 1, 2)
    rep = lambda v: jnp.repeat(v, SSD_HEAD_DIM).reshape(1, w)
    z_blk, xs_blk = 3 * sbw // w, (3 * sbw + w) // w
    b_blk, c_blk = (3 * sbw + 2 * w) // gn, (3 * sbw + 2 * w + gn) // gn
    assert 3 * sbw % w == 0 and (3 * sbw + 2 * w) % gn == 0
    full = lambda shape: pl.BlockSpec(shape, lambda b, c: (0,) * len(shape))
    return pl.pallas_call(
        functools.partial(_ssd_kernel, heads=heads),
        out_shape=jax.ShapeDtypeStruct((t, w), BF16),
        grid=(batch, nc),
        in_specs=[
            pl.BlockSpec((ln, w), lambda b, c: (b * nc + c, z_blk)),
            pl.BlockSpec((ln, w), lambda b, c: (b * nc + c, xs_blk)),
            pl.BlockSpec((ln, gn), lambda b, c: (b * nc + c, b_blk)),
            pl.BlockSpec((ln, gn), lambda b, c: (b * nc + c, c_blk)),
            pl.BlockSpec((1, ln, heads), lambda b, c: (b, c, 0)),
            pl.BlockSpec((1, heads, ln), lambda b, c: (b, 0, c)),
            full(conv_w.shape), full((1, ch)), full((1, heads)), full((heads, 1)),
            full((1, w)), full((heads, 1)), full((1, w)), full((1, w)),
        ],
        out_specs=pl.BlockSpec((ln, w), lambda b, c: (b * nc + c, 0)),
        scratch_shapes=[pltpu.VMEM((ln, ch), F32), pltpu.VMEM((SSD_GROUPS, SSD_STATE, w // SSD_GROUPS), F32)],
        compiler_params=_params(("arbitrary", "arbitrary")),
        name="ssd",
    )(proj, proj, proj, proj, dt3, dtt, conv_w, conv_b.reshape(1, ch), dt_bias.reshape(1, heads),
      dt_bias.reshape(heads, 1), rep(a_log), a_log.reshape(heads, 1), rep(d_skip), ssd_norm_w.reshape(1, w))


def _outproj_kernel(osb_ref, ossd_ref, x_ref, mod_ref, w1_ref, w2_ref, g_ref, b_ref, x1_ref, h2_ref, *, alpha):
    m = _dot(osb_ref[...], w1_ref[...]) + _dot(ossd_ref[...], w2_ref[...])
    v = alpha * x_ref[...] + (1.0 + mod_ref[0, 2:3, :]) * m
    x1 = _ln_stats(v) * g_ref[...] + b_ref[...]
    x1_ref[...] = x1
    h2_ref[...] = _pack_halves(_ln_stats(x1) * (1.0 + mod_ref[0, 4:5, :]) + mod_ref[0, 3:4, :])


def _outproj_call(o_sb, o_ssd, x2d, mod, w_out_b, ln_g, ln_b, seq, alpha):
    t, d = x2d.shape
    sbw, ssw = o_sb.shape[1], o_ssd.shape[1]
    tm = _pick(seq, (256, 128))
    per_b = seq // tm
    return pl.pallas_call(
        functools.partial(_outproj_kernel, alpha=alpha),
        out_shape=(jax.ShapeDtypeStruct((t, d), F32), jax.ShapeDtypeStruct((t, d // 2), jnp.uint32)),
        grid=(t // tm,),
        in_specs=[
            pl.BlockSpec((tm, sbw), lambda i: (i, 0)),
            pl.BlockSpec((tm, ssw), lambda i: (i, 0)),
            pl.BlockSpec((tm, d), lambda i: (i, 0)),
            pl.BlockSpec((1, 6, d), lambda i: (i // per_b, 0, 0)),
            pl.BlockSpec((sbw, d), lambda i: (0, 0)),
            pl.BlockSpec((ssw, d), lambda i: (0, 0)),
            pl.BlockSpec((1, d), lambda i: (0, 0)),
            pl.BlockSpec((1, d), lambda i: (0, 0)),
        ],
        out_specs=(pl.BlockSpec((tm, d), lambda i: (i, 0)), pl.BlockSpec((tm, d // 2), lambda i: (i, 0))),
        compiler_params=_params(("arbitrary",)),
        name="out_proj",
    )(o_sb, o_ssd, x2d, mod, w_out_b[:sbw], w_out_b[sbw:], ln_g.reshape(1, d), ln_b.reshape(1, d))


def _router_kernel(h_ref, wr_ref, bias_ref, idx_ref, rnk_ref, gate_ref, cnt_ref, carry_scr):
    ne = wr_ref.shape[0]
    tm = h_ref.shape[0]
    per_g = ne // N_EXPERT_GROUPS
    neg = -jnp.inf

    @pl.when(pl.program_id(0) == 0)
    def _():
        carry_scr[...] = jnp.zeros_like(carry_scr)

    ha, hb = _unpack_halves(h_ref[...])
    half = ha.shape[1]
    logits = _dot_nt(wr_ref[:, :half], ha.astype(BF16)) + _dot_nt(wr_ref[:, half:], hb.astype(BF16))
    scores = jax.nn.sigmoid(logits)
    biased = scores + bias_ref[...]
    s3 = scores.reshape(N_EXPERT_GROUPS, per_g, tm)
    b3 = biased.reshape(N_EXPERT_GROUPS, per_g, tm)
    shape3 = (N_EXPERT_GROUPS, per_g, tm)
    j_io = lax.broadcasted_iota(jnp.int32, shape3, 1)
    g_io = lax.broadcasted_iota(jnp.int32, shape3, 0)
    e_io = g_io * per_g + j_io

    m1 = jnp.max(b3, axis=1, keepdims=True)
    i1 = jnp.min(jnp.where(b3 == m1, j_io, per_g), axis=1, keepdims=True)
    m2 = jnp.max(jnp.where(j_io == i1, neg, b3), axis=1, keepdims=True)
    gs = m1 + m2
    gg = lax.broadcasted_iota(jnp.int32, gs.shape, 0)
    gmask = jnp.zeros(gs.shape, F32)
    cur = gs
    for _ in range(TOPK_GROUPS):
        m = jnp.max(cur, axis=0, keepdims=True)
        gi = jnp.min(jnp.where(cur == m, gg, N_EXPERT_GROUPS), axis=0, keepdims=True)
        pick = gg == gi
        gmask = jnp.where(pick, 1.0, gmask)
        cur = jnp.where(pick, neg, cur)
    cur = jnp.where(jnp.broadcast_to(gmask, shape3) > 0.5, b3, neg)

    sel = jnp.zeros(shape3, F32)
    picks = []
    for _ in range(TOP_K):
        m = jnp.max(jnp.max(cur, axis=1, keepdims=True), axis=0, keepdims=True)
        ei = jnp.min(jnp.min(jnp.where(cur == m, e_io, ne), axis=1, keepdims=True), axis=0, keepdims=True)
        pick = e_io == ei
        sel = jnp.where(pick, 1.0, sel)
        cur = jnp.where(pick, neg, cur)
        picks.append((pick, ei))

    selw = sel * s3
    denom = jnp.sum(jnp.sum(selw, axis=1, keepdims=True), axis=0, keepdims=True)
    gates3 = selw / denom * ROUTED_SCALE

    sel2 = sel.reshape(ne, tm)
    rj = lax.broadcasted_iota(jnp.int32, (tm, tm), 0)
    cs = lax.broadcasted_iota(jnp.int32, (tm, tm), 1)
    before = jnp.where(rj < cs, 1.0, 0.0).astype(BF16)
    carry = carry_scr[...]
    rank2 = _dot(sel2.astype(BF16), before) + carry[:, 0:1]
    rank3 = rank2.reshape(shape3)
    new_carry = carry + jnp.sum(sel2, axis=1, keepdims=True)
    carry_scr[...] = new_carry
    cnt_ref[...] = new_carry.astype(jnp.int32)

    for k, (pick, ei) in enumerate(picks):
        red = lambda v: jnp.sum(jnp.sum(v, axis=1, keepdims=True), axis=0, keepdims=True).reshape(1, tm)
        idx_ref[k:k + 1, :] = ei.reshape(1, tm)
        rnk_ref[k:k + 1, :] = red(jnp.where(pick, rank3, 0.0)).astype(jnp.int32)
        gate_ref[k:k + 1, :] = red(jnp.where(pick, gates3, 0.0))


def _router_call(h2p, wr_t, bias):
    t = h2p.shape[0]
    ne, d = wr_t.shape
    tm = _pick(t, (512, 256))
    if t == tm:
        tm = tm // 2
    return pl.pallas_call(
        _router_kernel,
        out_shape=(jax.ShapeDtypeStruct((TOP_K, t), jnp.int32), jax.ShapeDtypeStruct((TOP_K, t), jnp.int32),
                   jax.ShapeDtypeStruct((TOP_K, t), F32), jax.ShapeDtypeStruct((ne, 128), jnp.int32)),
        grid=(t // tm,),
        in_specs=[
            pl.BlockSpec((tm, d // 2), lambda i: (i, 0)),
            pl.BlockSpec((ne, d), lambda i: (0, 0)),
            pl.BlockSpec((ne, 1), lambda i: (0, 0)),
        ],
        out_specs=(pl.BlockSpec((TOP_K, tm), lambda i: (0, i)), pl.BlockSpec((TOP_K, tm), lambda i: (0, i)),
                   pl.BlockSpec((TOP_K, tm), lambda i: (0, i)), pl.BlockSpec((ne, 128), lambda i: (0, 0))),
        scratch_shapes=[pltpu.VMEM((ne, 128), F32)],
        compiler_params=_params(("arbitrary",)),
        name="router",
    )(h2p, wr_t, bias.reshape(ne, 1))


def _dispatch_kernel(ps_ref, pc_ref, nu_ref, dest_ref, h_ref, xs_hbm, zbuf, idx_smem, isem, ssem, zsem, tsem):
    tm = h_ref.shape[0]
    ne = ps_ref.shape[0]
    tb = zbuf.shape[0]
    nblk = xs_hbm.shape[0] // tb

    @pl.when(pl.program_id(0) == 0)
    def _():
        zbuf[...] = jnp.zeros_like(zbuf)

        def tail(r, c):
            pltpu.make_async_copy(zbuf, xs_hbm.at[pl.ds(pl.multiple_of(r * tb, tb), tb)], tsem).start()
            return c

        def tail_drain(r, c):
            pltpu.make_async_copy(zbuf, xs_hbm.at[pl.ds(0, tb)], tsem).wait()
            return c

        lax.fori_loop(nu_ref[0], nblk, tail, 0)
        lax.fori_loop(nu_ref[0], nblk, tail_drain, 0)

        def fill(e, c):
            def one(j, c2):
                pltpu.make_async_copy(zbuf.at[0], xs_hbm.at[ps_ref[e] + j], zsem).start()
                return c2
            return lax.fori_loop(0, pc_ref[e], one, c)

        def drain(e, c):
            def one(j, c2):
                pltpu.make_async_copy(zbuf.at[0], xs_hbm.at[0], zsem).wait()
                return c2
            return lax.fori_loop(0, pc_ref[e], one, c)

        lax.fori_loop(0, ne, fill, 0)
        lax.fori_loop(0, ne, drain, 0)

    cp = pltpu.make_async_copy(dest_ref.at[0, 0], idx_smem, isem)
    cp.start()
    cp.wait()

    def issue(i, c):
        for k in range(TOP_K):
            pltpu.make_async_copy(h_ref.at[i], xs_hbm.at[idx_smem[k * tm + i]], ssem).start(priority=k % 2)
        return c

    lax.fori_loop(0, tm, issue, 0, unroll=2)
    pltpu.make_async_copy(xs_hbm.at[pl.ds(0, TOP_K * tm)], xs_hbm.at[pl.ds(0, TOP_K * tm)], ssem).wait()


def _dispatch_call(pad_start, pad_cnt, n_used, dest, h2p, n_rows, tb):
    t, d = h2p.shape
    tm = _pick(t, (512, 256))
    nt = t // tm
    dest_t = dest.reshape(TOP_K, nt, tm).transpose(1, 0, 2).reshape(nt, 1, TOP_K * tm)
    grid_spec = pltpu.PrefetchScalarGridSpec(
        num_scalar_prefetch=3,
        grid=(nt,),
        in_specs=[
            pl.BlockSpec((1, 1, TOP_K * tm), lambda i, ps, pc, nu: (i, 0, 0)),
            pl.BlockSpec((tm, d), lambda i, ps, pc, nu: (i, 0)),
        ],
        out_specs=pl.BlockSpec(memory_space=pl.ANY),
        scratch_shapes=[pltpu.VMEM((tb, d), h2p.dtype), pltpu.SMEM((TOP_K * tm,), jnp.int32), pltpu.SemaphoreType.DMA,
                        pltpu.SemaphoreType.DMA, pltpu.SemaphoreType.DMA, pltpu.SemaphoreType.DMA],
    )
    return pl.pallas_call(
        _dispatch_kernel,
        out_shape=jax.ShapeDtypeStruct((n_rows, d), h2p.dtype),
        grid_spec=grid_spec,
        compiler_params=_params(("arbitrary",)),
        name="dispatch",
    )(pad_start, pad_cnt, n_used, dest_t, h2p)


def _expert_kernel(be_ref, nu_ref, x_ref, wg_ref, wu_ref, wd_ref, o_ref, wgb, wub, wdb):
    r = pl.program_id(0)

    @pl.when(r < nu_ref[0])
    def _():
        @pl.when(jnp.logical_or(r == 0, be_ref[r] != be_ref[jnp.maximum(r - 1, 0)]))
        def _():
            wgb[...] = wg_ref[0, 0].astype(BF16)
            wub[...] = wu_ref[0, 0].astype(BF16)
            wdb[...] = wd_ref[0, 0].astype(BF16)

        xa, xb = _unpack_halves(x_ref[...])
        xa, xb = xa.astype(BF16), xb.astype(BF16)
        half = xa.shape[1]
        g = _dot(xa, wgb[:half, :]) + _dot(xb, wgb[half:, :])
        u = _dot(xa, wub[:half, :]) + _dot(xb, wub[half:, :])
        o_ref[...] = _pack_halves(_dot((_silu(g) * u).astype(BF16), wdb[...]))

    @pl.when(r >= nu_ref[0])
    def _():
        o_ref[...] = jnp.zeros_like(o_ref)


def _expert_call(block_expert, n_used, xs, w_gate, w_up, w_down, layer, tb):
    n_rows = xs.shape[0]
    d, de = w_gate.shape[2:]
    nblk = n_rows // tb
    blk = lambda r, be, nu: jnp.minimum(r, nu[0] - 1)
    wmap = lambda r, be, nu: (layer, be[blk(r, be, nu)], 0, 0)
    grid_spec = pltpu.PrefetchScalarGridSpec(
        num_scalar_prefetch=2,
        grid=(nblk,),
        in_specs=[
            pl.BlockSpec((tb, d // 2), lambda r, be, nu: (blk(r, be, nu), 0)),
            pl.BlockSpec((1, 1, d, de), wmap),
            pl.BlockSpec((1, 1, d, de), wmap),
            pl.BlockSpec((1, 1, de, d), wmap),
        ],
        out_specs=pl.BlockSpec((tb, d // 2), lambda r, be, nu: (r, 0)),
        scratch_shapes=[pltpu.VMEM((d, de), BF16), pltpu.VMEM((d, de), BF16), pltpu.VMEM((de, d), BF16)],
    )
    return pl.pallas_call(
        _expert_kernel,
        out_shape=jax.ShapeDtypeStruct((n_rows, d // 2), jnp.uint32),
        grid_spec=grid_spec,
        compiler_params=_params(("arbitrary",)),
        name="experts",
    )(block_expert, n_used, xs, w_gate, w_up, w_down)


def _combine_kernel(pos_ref, posn_ref, gate_ref, h_ref, x_ref, mod_ref, o_hbm, sg_ref, su_ref, sd_ref, g_ref, b_ref,
                    out_ref, gbuf, idx_smem, isem, gsem, *, alpha):
    i = pl.program_id(0)
    tm = h_ref.shape[0]
    n = TOP_K * tm

    def gather(src_ref, slot):
        cp = pltpu.make_async_copy(src_ref.at[0, 0], idx_smem, isem)
        cp.start()
        cp.wait()

        def issue(j, c):
            for p in range(2):
                r = 2 * j + p
                pltpu.make_async_copy(o_hbm.at[idx_smem[r]], gbuf.at[slot, r], gsem.at[slot]).start(priority=p)
            return c

        lax.fori_loop(0, n // 2, issue, 0, unroll=4)

    @pl.when(i == 0)
    def _():
        gather(pos_ref, 0)

    @pl.when(i + 1 < pl.num_programs(0))
    def _():
        gather(posn_ref, (i + 1) % 2)

    ha, hb = _unpack_halves(h_ref[...])
    ha, hb = ha.astype(BF16), hb.astype(BF16)
    half = ha.shape[1]
    sg = _dot(ha, sg_ref[:half, :]) + _dot(hb, sg_ref[half:, :])
    su = _dot(ha, su_ref[:half, :]) + _dot(hb, su_ref[half:, :])
    f = _dot((_silu(sg) * su).astype(BF16), sd_ref[...])
    fa, fb = f[:, :half], f[:, half:]
    slot = i % 2
    pltpu.make_async_copy(o_hbm.at[pl.ds(0, n)], gbuf.at[slot], gsem.at[slot]).wait()
    gate = gate_ref[...]
    for k in range(TOP_K):
        oa, ob = _unpack_halves(gbuf[slot, pl.ds(k * tm, tm), :])
        fa = fa + oa * gate[:, k:k + 1]
        fb = fb + ob * gate[:, k:k + 1]
    f = jnp.concatenate([fa, fb], axis=1)
    v = alpha * x_ref[...] + (1.0 + mod_ref[0, 5:6, :]) * f
    out_ref[...] = _ln_stats(v) * g_ref[...] + b_ref[...]


def _combine_call(pos, gate_t, h2p, x1, mod, o_sorted, sg, su, sd, ln_g, ln_b, seq, alpha):
    t, d = x1.shape
    ds = sg.shape[1]
    tm = pos.shape[2] // TOP_K
    per_b = seq // tm
    nt = t // tm
    return pl.pallas_call(
        functools.partial(_combine_kernel, alpha=alpha),
        out_shape=jax.ShapeDtypeStruct((t, d), F32),
        grid=(nt,),
        in_specs=[
            pl.BlockSpec((1, 1, TOP_K * tm), lambda i: (i, 0, 0)),
            pl.BlockSpec((1, 1, TOP_K * tm), lambda i: (jnp.minimum(i + 1, nt - 1), 0, 0)),
            pl.BlockSpec((tm, TOP_K), lambda i: (i, 0)),
            pl.BlockSpec((tm, d // 2), lambda i: (i, 0)),
            pl.BlockSpec((tm, d), lambda i: (i, 0)),
            pl.BlockSpec((1, 6, d), lambda i: (i // per_b, 0, 0)),
            pl.BlockSpec(memory_space=pl.ANY),
            pl.BlockSpec((d, ds), lambda i: (0, 0)),
            pl.BlockSpec((d, ds), lambda i: (0, 0)),
            pl.BlockSpec((ds, d), lambda i: (0, 0)),
            pl.BlockSpec((1, d), lambda i: (0, 0)),
            pl.BlockSpec((1, d), lambda i: (0, 0)),
        ],
        out_specs=pl.BlockSpec((tm, d), lambda i: (i, 0)),
        scratch_shapes=[pltpu.VMEM((2, TOP_K * tm, d // 2), jnp.uint32), pltpu.SMEM((TOP_K * tm,), jnp.int32),
                        pltpu.SemaphoreType.DMA, pltpu.SemaphoreType.DMA((2,))],
        compiler_params=_params(("arbitrary",)),
        name="combine",
    )(pos, pos, gate_t, h2p, x1, mod, o_sorted, sg, su, sd, ln_g.reshape(1, d), ln_b.reshape(1, d))


def _moe(h2, x1, mod, wr_t, router_bias, w_gate, w_up, w_down, layer, sg, su, sd, ln_g, ln_b, seq, alpha, tb):
    t, d = x1.shape
    ne = w_gate.shape[1]
    idx, rnk, gate, cnt = _router_call(h2, wr_t, router_bias)
    counts = cnt[:, 0]
    padded = (counts + tb - 1) // tb * tb
    pend = jnp.cumsum(padded)
    pstart = pend - padded
    eids = jnp.arange(ne, dtype=jnp.int32)
    dest = rnk + jnp.sum(jnp.where(idx[..., None] == eids, pstart.astype(jnp.int32), 0), axis=-1)
    nblk = (t * TOP_K) // tb + ne
    blk_row = jnp.arange(nblk, dtype=jnp.int32) * tb
    block_expert = jnp.minimum(jnp.sum((pend[None, :] <= blk_row[:, None]).astype(jnp.int32), axis=1), ne - 1)
    n_used = (pend[-1] // tb).astype(jnp.int32).reshape(1)
    xs = _dispatch_call((pstart + counts).astype(jnp.int32), (padded - counts).astype(jnp.int32), n_used, dest, h2,
                        nblk * tb, tb)
    o_sorted = _expert_call(block_expert, n_used, xs, w_gate, w_up, w_down, layer, tb)
    tm = _pick(seq, (256, 128))
    pos = dest.reshape(TOP_K, t // tm, tm).transpose(1, 0, 2).reshape(t // tm, 1, TOP_K * tm)
    return _combine_call(pos, gate.T, h2, x1, mod, o_sorted, sg, su, sd, ln_g, ln_b, seq, alpha)


def kernel(x, c, w_ada, b_ada, w_in, conv_w, conv_b, dt_bias, a_log, d_skip, sb_norm_w, ssd_norm_w, w_out, ln1_g,
           ln1_b, w_router, router_bias, w_gate, w_up, w_down, ws_gate, ws_up, ws_down, ln2_g, ln2_b):
    batch, seq, d = x.shape
    depth = w_ada.shape[0]
    sbw = sb_norm_w.shape[1]
    sb_heads = sbw // SB_HEAD_DIM
    ssd_heads = dt_bias.shape[1]
    n_main = w_in.shape[2] - ssd_heads
    alpha = (2 * depth) ** 0.25
    tb = 512 if (batch * seq * TOP_K) // w_gate.shape[1] >= 2048 else 128

    mod_all = _ada_call(c, w_ada, b_ada).reshape(depth, batch, 6, d)
    x2d = x.reshape(batch * seq, d)
    for l in range(depth):
        mod = mod_all[l]
        w_main = w_in[l, :, :n_main].astype(BF16)
        w_dt = jnp.pad(w_in[l, :, n_main:], ((0, 0), (0, 128 - ssd_heads))).astype(BF16)
        proj, dtp = _inproj_call(x2d, mod, w_main, w_dt, seq)
        o_sb = _sb_call(proj, sb_norm_w[l], batch, seq, sb_heads)
        o_ssd = _ssd_call(proj, dtp[:, :ssd_heads], conv_w[l], conv_b[l], dt_bias[l], a_log[l], d_skip[l],
                          ssd_norm_w[l], batch, seq, sbw)
        x1, h2 = _outproj_call(o_sb, o_ssd, x2d, mod, w_out[l].astype(BF16), ln1_g[l], ln1_b[l], seq, alpha)
        x2d = _moe(h2, x1, mod, w_router[l].T.astype(BF16), router_bias[l], w_gate, w_up, w_down, l,
                   ws_gate[l].astype(BF16), ws_up[l].astype(BF16), ws_down[l].astype(BF16), ln2_g[l], ln2_b[l],
                   seq, alpha, tb)
    return x2d.reshape(batch, seq, d)
```

```python
import functools
import math

import jax
import jax.numpy as jnp
from jax import lax
from jax.experimental import pallas as pl
from jax.experimental.pallas import tpu as pltpu

F32 = jnp.float32
BF16 = jnp.bfloat16

SB_HEAD_DIM = 128
SSD_HEAD_DIM = 64
SSD_GROUPS = 2
SSD_STATE = 128
SSD_CHUNK = 128
TOP_K = 8
N_EXPERT_GROUPS = 8
TOPK_GROUPS = 4
ROUTED_SCALE = 2.5
LN_EPS = 1e-5
RMS_EPS = 1e-6
SB_DEAD_LOG = -110.0
META_W = 128
META_TOK = 1
META_IDX = 8

VMEM_LIMIT_BYTES = 56 * 1024 * 1024


def _params(sem, vmem=VMEM_LIMIT_BYTES):
    return pltpu.CompilerParams(dimension_semantics=sem, vmem_limit_bytes=vmem)


def _pick(n, cands):
    for c in cands:
        if n % c == 0:
            return c
    raise ValueError(f"no tile in {cands} divides {n}")


def _softplus(z):
    return jnp.maximum(z, 0.0) + jnp.log(1.0 + jnp.exp(-jnp.abs(z)))


def _silu(v):
    return v * jax.nn.sigmoid(v)


def _split3(v):
    hi = v.astype(BF16)
    r1 = v - hi.astype(F32)
    mid = r1.astype(BF16)
    lo = (r1 - mid.astype(F32)).astype(BF16)
    return hi, mid, lo


def _dot(a, b):
    return jnp.dot(a, b, preferred_element_type=F32)


def _dot_nt(a, b):
    return lax.dot_general(a, b, (((1,), (1,)), ((), ())), preferred_element_type=F32)


def _dot_tn(a, b):
    return lax.dot_general(a, b, (((0,), (0,)), ((), ())), preferred_element_type=F32)


def _pack_halves(v):
    half = v.shape[1] // 2
    hi = lax.bitcast_convert_type(v[:, :half].astype(BF16).astype(F32), jnp.uint32)
    lo = lax.bitcast_convert_type(v[:, half:].astype(BF16).astype(F32), jnp.uint32)
    return hi | (lo >> 16)


def _unpack_halves(w):
    hi = lax.bitcast_convert_type(w & jnp.uint32(0xFFFF0000), F32)
    lo = lax.bitcast_convert_type(w << 16, F32)
    return hi, lo


def _ln_stats(v):
    mu = jnp.mean(v, axis=-1, keepdims=True)
    d = v - mu
    var = jnp.mean(d * d, axis=-1, keepdims=True)
    return d * lax.rsqrt(var + LN_EPS)


def _ada_kernel(c_ref, w_ref, b_ref, o_ref):
    ca = _silu(c_ref[...]).astype(BF16)
    o_ref[0] = _dot(ca, w_ref[0].astype(BF16)) + b_ref[0]


def _ada_call(c, w_ada, b_ada):
    depth, d, n = w_ada.shape
    b = c.shape[0]
    tn = _pick(n, (512, 256, 128))
    return pl.pallas_call(
        _ada_kernel,
        out_shape=jax.ShapeDtypeStruct((depth, b, n), F32),
        grid=(depth, n // tn),
        in_specs=[
            pl.BlockSpec((b, d), lambda l, j: (0, 0)),
            pl.BlockSpec((1, d, tn), lambda l, j: (l, 0, j)),
            pl.BlockSpec((1, 1, tn), lambda l, j: (l, 0, j)),
        ],
        out_specs=pl.BlockSpec((1, b, tn), lambda l, j: (l, 0, j)),
        compiler_params=_params(("arbitrary", "arbitrary")),
        name="ada_mod",
    )(c, w_ada, b_ada.reshape(depth, 1, n))


def _inproj_kernel(x_ref, mod_ref, w_ref, wdt_ref, o_ref, dt_ref, h_scr):
    @pl.when(pl.program_id(1) == 0)
    def _():
        xn = _ln_stats(x_ref[...])
        h = xn * (1.0 + mod_ref[0, 1:2, :]) + mod_ref[0, 0:1, :]
        hb = h.astype(BF16)
        h_scr[...] = hb
        dt_ref[...] = _dot(hb, wdt_ref[...])

    o_ref[...] = _dot(h_scr[...], w_ref[...]).astype(BF16)


def _inproj_call(x2d, mod, w_main, w_dt, seq):
    t, d = x2d.shape
    n = w_main.shape[1]
    tm = _pick(seq, (1024, 512, 256, 128))
    tn = _pick(n, (512, 256, 128))
    per_b = seq // tm
    return pl.pallas_call(
        _inproj_kernel,
        out_shape=(jax.ShapeDtypeStruct((t, n), BF16), jax.ShapeDtypeStruct((t, 128), F32)),
        grid=(t // tm, n // tn),
        in_specs=[
            pl.BlockSpec((tm, d), lambda i, j: (i, 0)),
            pl.BlockSpec((1, 6, d), lambda i, j: (i // per_b, 0, 0)),
            pl.BlockSpec((d, tn), lambda i, j: (0, j)),
            pl.BlockSpec((d, 128), lambda i, j: (0, 0)),
        ],
        out_specs=(
            pl.BlockSpec((tm, tn), lambda i, j: (i, j)),
            pl.BlockSpec((tm, 128), lambda i, j: (i, 0)),
        ),
        scratch_shapes=[pltpu.VMEM((tm, d), BF16)],
        compiler_params=_params(("arbitrary", "arbitrary")),
        name="in_proj",
    )(x2d, mod, w_main, w_dt)


def _sb_kernel(q_ref, k_ref, v_ref, nw_ref, o_ref, *, tq, tk, hg):
    qi = pl.program_id(2)
    hd = SB_HEAD_DIM
    scale = 1.0 / math.sqrt(hd)
    nd = tq // tk
    rj = lax.broadcasted_iota(jnp.int32, (tq, tk), 0)
    cs = lax.broadcasted_iota(jnp.int32, (tq, tk), 1)
    lr = lax.broadcasted_iota(jnp.int32, (tk, tk), 0)
    lc = lax.broadcasted_iota(jnp.int32, (tk, tk), 1)
    later = jnp.where(lr > lc, 1.0, 0.0).astype(BF16)
    qs = [q_ref[:, g * hd:(g + 1) * hd] for g in range(hg)]

    def block(kb, accs, runs, offset):
        start = pl.multiple_of(kb * tk, tk)
        new_accs, new_runs = [], []
        for g in range(hg):
            k = k_ref[pl.ds(start, tk), g * hd:(g + 1) * hd]
            v = v_ref[pl.ds(start, tk), g * hd:(g + 1) * hd]
            z = _dot_nt(qs[g], k) * scale
            sp = _softplus(z)
            if offset is None:
                lom = -sp
            else:
                mask = (cs + offset) < rj
                lom = jnp.where(mask, -sp, 0.0)
            hi = lom.astype(BF16)
            lo = (lom - hi.astype(F32)).astype(BF16)
            between = _dot(hi, later) + _dot(lo, later) + runs[g]
            w = jnp.exp(z - sp + between)
            if offset is not None:
                w = jnp.where(mask, w, 0.0)
            new_accs.append(accs[g] + _dot(w.astype(BF16), v))
            new_runs.append(runs[g] + jnp.sum(lom, axis=1, keepdims=True))
        return new_accs, new_runs

    accs = [jnp.zeros((tq, hd), F32) for _ in range(hg)]
    runs = [jnp.zeros((tq, 1), F32) for _ in range(hg)]
    for j in range(nd):
        accs, runs = block(qi * nd + (nd - 1 - j), accs, runs, (nd - 1 - j) * tk)

    def run_max(rs):
        m = rs[0]
        for r in rs[1:]:
            m = jnp.maximum(m, r)
        return jnp.max(m)

    def cond(carry):
        i, top, _, _ = carry
        return jnp.logical_and(i < qi * nd, top > SB_DEAD_LOG)

    def body(carry):
        i, _, accs, runs = carry
        accs, runs = block(qi * nd - 1 - i, list(accs), list(runs), None)
        return i + 1, run_max(runs), tuple(accs), tuple(runs)

    _, _, accs, _ = lax.while_loop(cond, body, (jnp.int32(0), run_max(runs), tuple(accs), tuple(runs)))
    for g in range(hg):
        acc = accs[g]
        o = acc * lax.rsqrt(jnp.mean(acc * acc, axis=-1, keepdims=True) + RMS_EPS) * nw_ref[:, g * hd:(g + 1) * hd]
        o_ref[:, g * hd:(g + 1) * hd] = o.astype(BF16)


def _sb_call(proj, sb_norm_w, batch, seq, heads):
    t = proj.shape[0]
    tq = _pick(seq, (256, 128))
    tk = tq
    hg = 4 if heads % 4 == 0 else 2
    nq = seq // tq
    hb = heads // hg
    wd = hg * SB_HEAD_DIM
    return pl.pallas_call(
        functools.partial(_sb_kernel, tq=tq, tk=tk, hg=hg),
        out_shape=jax.ShapeDtypeStruct((t, heads * SB_HEAD_DIM), BF16),
        grid=(batch, hb, nq),
        in_specs=[
            pl.BlockSpec((tq, wd), lambda b, h, i: (b * nq + i, h)),
            pl.BlockSpec((seq, wd), lambda b, h, i: (b, hb + h)),
            pl.BlockSpec((seq, wd), lambda b, h, i: (b, 2 * hb + h)),
            pl.BlockSpec((1, wd), lambda b, h, i: (0, h)),
        ],
        out_specs=pl.BlockSpec((tq, wd), lambda b, h, i: (b * nq + i, h)),
        compiler_params=_params(("arbitrary", "arbitrary", "arbitrary")),
        name="sb_attn",
    )(proj, proj, proj, sb_norm_w.reshape(1, -1))


def _ssd_kernel(z_ref, xs_ref, b_ref, c_ref, dt_ref, dtt_ref, cw_ref, cb_ref, dtb_ref, dtbt_ref,
                alog_ref, alogt_ref, dskip_ref, nw_ref, o_ref, prev_scr, state_scr, *, heads):
    ln = xs_ref.shape[0]
    w = xs_ref.shape[1]
    gn = b_ref.shape[1]
    n = gn // SSD_GROUPS
    wg = w // SSD_GROUPS
    hpg = heads // SSD_GROUPS

    @pl.when(pl.program_id(1) == 0)
    def _():
        prev_scr[...] = jnp.zeros_like(prev_scr)
        state_scr[...] = jnp.zeros_like(state_scr)

    raw = jnp.concatenate([xs_ref[...], b_ref[...], c_ref[...]], axis=1).astype(F32)
    pr = prev_scr[...]
    ch = raw.shape[1]
    row = lax.broadcasted_iota(jnp.int32, (ln, ch), 0)
    taps = cw_ref.shape[0]
    acc = raw * cw_ref[taps - 1:taps, :] + cb_ref[...]
    for sh in range(1, taps):
        shifted = jnp.where(row < sh, pltpu.roll(pr, sh, 0), pltpu.roll(raw, sh, 0))
        acc = acc + shifted * cw_ref[taps - 1 - sh:taps - sh, :]
    prev_scr[...] = raw
    xbc = _silu(acc)
    xs = xbc[:, :w]
    bm = xbc[:, w:w + gn]
    cm = xbc[:, w + gn:]

    dtv = _softplus(dt_ref[0] + dtb_ref[...])
    dtvt = _softplus(dtt_ref[0] + dtbt_ref[...])
    a_full = -jnp.exp(alog_ref[...])
    a_t = -jnp.exp(alogt_ref[...])

    eh = lax.broadcasted_iota(jnp.int32, (heads, w), 0)
    ec = lax.broadcasted_iota(jnp.int32, (heads, w), 1)
    expand = jnp.where(ec // SSD_HEAD_DIM == eh, 1.0, 0.0).astype(BF16)
    d_hi, d_mid, d_lo = _split3(dtv)
    dt_exp = _dot(d_hi, expand) + _dot(d_mid, expand) + _dot(d_lo, expand)

    rl = lax.broadcasted_iota(jnp.int32, (ln, ln), 0)
    cl = lax.broadcasted_iota(jnp.int32, (ln, ln), 1)
    tril = cl <= rl
    incl = jnp.where(tril, 1.0, 0.0).astype(BF16)
    incl_t = jnp.where(rl <= cl, 1.0, 0.0).astype(BF16)

    a_exp = dt_exp * a_full
    a_hi, a_mid, a_lo = _split3(a_exp)
    acum = _dot(incl, a_hi) + _dot(incl, a_mid) + _dot(incl, a_lo)
    at = dtvt * a_t
    t_hi, t_mid, t_lo = _split3(at)
    acum_t = _dot(t_hi, incl_t) + _dot(t_mid, incl_t) + _dot(t_lo, incl_t)

    xdt = xs * dt_exp
    last = acum[ln - 1:ln, :]
    xdd = (xdt * jnp.exp(last - acum)).astype(BF16)
    xdtb = xdt.astype(BF16)
    ea = jnp.exp(acum)
    cdec = jnp.exp(last)

    col_head = lax.broadcasted_iota(jnp.int32, (ln, wg), 1) // SSD_HEAD_DIM
    ys = []
    for g in range(SSD_GROUPS):
        bg = bm[:, g * n:(g + 1) * n].astype(BF16)
        cg = cm[:, g * n:(g + 1) * n].astype(BF16)
        cb = _dot_nt(cg, bg)
        ms = []
        for e in range(hpg):
            h = g * hpg + e
            ac = acum[:, h * SSD_HEAD_DIM:h * SSD_HEAD_DIM + 1]
            ar = acum_t[h:h + 1, :]
            lmat = jnp.exp(jnp.where(tril, ac - ar, -jnp.inf))
            ms.append((cb * lmat).astype(BF16))
        mcat = jnp.concatenate(ms, axis=1)
        xg = xdtb[:, g * wg:(g + 1) * wg]
        xbd = jnp.concatenate([jnp.where(col_head == e, xg, jnp.zeros_like(xg)) for e in range(hpg)], axis=0)
        y_diag = _dot(mcat, xbd)
        sg = state_scr[g]
        y_off = _dot(cg, sg.astype(BF16)) * ea[:, g * wg:(g + 1) * wg]
        s_new = _dot_tn(bg, xdd[:, g * wg:(g + 1) * wg])
        state_scr[g] = sg * cdec[:, g * wg:(g + 1) * wg] + s_new
        ys.append(y_diag + y_off)
    y = jnp.concatenate(ys, axis=1) + xs * dskip_ref[...]
    y = y * _silu(z_ref[...].astype(F32))
    o = y * lax.rsqrt(jnp.mean(y * y, axis=-1, keepdims=True) + RMS_EPS) * nw_ref[...]
    o_ref[...] = o.astype(BF16)


def _ssd_call(proj, dt, conv_w, conv_b, dt_bias, a_log, d_skip, ssd_norm_w, batch, seq, sbw):
    t = proj.shape[0]
    heads = dt_bias.shape[0]
    w = heads * SSD_HEAD_DIM
    gn = SSD_GROUPS * SSD_STATE
    ch = w + 2 * gn
    ln = SSD_CHUNK
    nc = seq // ln
    dt3 = dt.reshape(batch, seq, heads)
    dtt = jnp.swapaxes(dt3, 1, 2)
    rep = lambda v: jnp.repeat(v, SSD_HEAD_DIM).reshape(1, w)
    z_blk, xs_blk = 3 * sbw // w, (3 * sbw + w) // w
    b_blk, c_blk = (3 * sbw + 2 * w) // gn, (3 * sbw + 2 * w + gn) // gn
    assert 3 * sbw % w == 0 and (3 * sbw + 2 * w) % gn == 0
    full = lambda shape: pl.BlockSpec(shape, lambda b, c: (0,) * len(shape))
    return pl.pallas_call(
        functools.partial(_ssd_kernel, heads=heads),
        out_shape=jax.ShapeDtypeStruct((t, w), BF16),
        grid=(batch, nc),
        in_specs=[
            pl.BlockSpec((ln, w), lambda b, c: (b * nc + c, z_blk)),
            pl.BlockSpec((ln, w), lambda b, c: (b * nc + c, xs_blk)),
            pl.BlockSpec((ln, gn), lambda b, c: (b * nc + c, b_blk)),
            pl.BlockSpec((ln, gn), lambda b, c: (b * nc + c, c_blk)),
            pl.BlockSpec((1, ln, heads), lambda b, c: (b, c, 0)),
            pl.BlockSpec((1, heads, ln), lambda b, c: (b, 0, c)),
            full(conv_w.shape), full((1, ch)), full((1, heads)), full((heads, 1)),
            full((1, w)), full((heads, 1)), full((1, w)), full((1, w)),
        ],
        out_specs=pl.BlockSpec((ln, w), lambda b, c: (b * nc + c, 0)),
        scratch_shapes=[pltpu.VMEM((ln, ch), F32), pltpu.VMEM((SSD_GROUPS, SSD_STATE, w // SSD_GROUPS), F32)],
        compiler_params=_params(("arbitrary", "arbitrary")),
        name="ssd",
    )(proj, proj, proj, proj, dt3, dtt, conv_w, conv_b.reshape(1, ch), dt_bias.reshape(1, heads),
      dt_bias.reshape(heads, 1), rep(a_log), a_log.reshape(heads, 1), rep(d_skip), ssd_norm_w.reshape(1, w))


def _outproj_kernel(osb_ref, ossd_ref, x_ref, mod_ref, w1_ref, w2_ref, g_ref, b_ref, x1_ref, h2_ref, *, alpha):
    m = _dot(osb_ref[...], w1_ref[...]) + _dot(ossd_ref[...], w2_ref[...])
    v = alpha * x_ref[...] + (1.0 + mod_ref[0, 2:3, :]) * m
    x1 = _ln_stats(v) * g_ref[...] + b_ref[...]
    x1_ref[...] = x1
    h2_ref[...] = _pack_halves(_ln_stats(x1) * (1.0 + mod_ref[0, 4:5, :]) + mod_ref[0, 3:4, :])


def _outproj_call(o_sb, o_ssd, x2d, mod, w_out_b, ln_g, ln_b, seq, alpha):
    t, d = x2d.shape
    sbw, ssw = o_sb.shape[1], o_ssd.shape[1]
    tm = _pick(seq, (256, 128))
    per_b = seq // tm
    return pl.pallas_call(
        functools.partial(_outproj_kernel, alpha=alpha),
        out_shape=(jax.ShapeDtypeStruct((t, d), F32), jax.ShapeDtypeStruct((t, d // 2), jnp.uint32)),
        grid=(t // tm,),
        in_specs=[
            pl.BlockSpec((tm, sbw), lambda i: (i, 0)),
            pl.BlockSpec((tm, ssw), lambda i: (i, 0)),
            pl.BlockSpec((tm, d), lambda i: (i, 0)),
            pl.BlockSpec((1, 6, d), lambda i: (i // per_b, 0, 0)),
            pl.BlockSpec((sbw, d), lambda i: (0, 0)),
            pl.BlockSpec((ssw, d), lambda i: (0, 0)),
            pl.BlockSpec((1, d), lambda i: (0, 0)),
            pl.BlockSpec((1, d), lambda i: (0, 0)),
        ],
        out_specs=(pl.BlockSpec((tm, d), lambda i: (i, 0)), pl.BlockSpec((tm, d // 2), lambda i: (i, 0))),
        compiler_params=_params(("arbitrary",)),
        name="out_proj",
    )(o_sb, o_ssd, x2d, mod, w_out_b[:sbw], w_out_b[sbw:], ln_g.reshape(1, d), ln_b.reshape(1, d))


def _router_kernel(h_ref, wr_ref, bias_ref, idx_ref, rnk_ref, gate_ref, cnt_ref, carry_scr):
    ne = wr_ref.shape[0]
    tm = h_ref.shape[0]
    per_g = ne // N_EXPERT_GROUPS
    neg = -jnp.inf

    @pl.when(pl.program_id(0) == 0)
    def _():
        carry_scr[...] = jnp.zeros_like(carry_scr)

    ha, hb = _unpack_halves(h_ref[...])
    half = ha.shape[1]
    logits = _dot_nt(wr_ref[:, :half], ha.astype(BF16)) + _dot_nt(wr_ref[:, half:], hb.astype(BF16))
    scores = jax.nn.sigmoid(logits)
    biased = scores + bias_ref[...]
    s3 = scores.reshape(N_EXPERT_GROUPS, per_g, tm)
    b3 = biased.reshape(N_EXPERT_GROUPS, per_g, tm)
    shape3 = (N_EXPERT_GROUPS, per_g, tm)
    j_io = lax.broadcasted_iota(jnp.int32, shape3, 1)
    g_io = lax.broadcasted_iota(jnp.int32, shape3, 0)
    e_io = g_io * per_g + j_io

    m1 = jnp.max(b3, axis=1, keepdims=True)
    i1 = jnp.min(jnp.where(b3 == m1, j_io, per_g), axis=1, keepdims=True)
    m2 = jnp.max(jnp.where(j_io == i1, neg, b3), axis=1, keepdims=True)
    gs = m1 + m2
    gg = lax.broadcasted_iota(jnp.int32, gs.shape, 0)
    gmask = jnp.zeros(gs.shape, F32)
    cur = gs
    for _ in range(TOPK_GROUPS):
        m = jnp.max(cur, axis=0, keepdims=True)
        gi = jnp.min(jnp.where(cur == m, gg, N_EXPERT_GROUPS), axis=0, keepdims=True)
        pick = gg == gi
        gmask = jnp.where(pick, 1.0, gmask)
        cur = jnp.where(pick, neg, cur)
    cur = jnp.where(jnp.broadcast_to(gmask, shape3) > 0.5, b3, neg)

    sel = jnp.zeros(shape3, F32)
    picks = []
    for _ in range(TOP_K):
        m = jnp.max(jnp.max(cur, axis=1, keepdims=True), axis=0, keepdims=True)
        ei = jnp.min(jnp.min(jnp.where(cur == m, e_io, ne), axis=1, keepdims=True), axis=0, keepdims=True)
        pick = e_io == ei
        sel = jnp.where(pick, 1.0, sel)
        cur = jnp.where(pick, neg, cur)
        picks.append((pick, ei))

    selw = sel * s3
    denom = jnp.sum(jnp.sum(selw, axis=1, keepdims=True), axis=0, keepdims=True)
    gates3 = selw / denom * ROUTED_SCALE

    sel2 = sel.reshape(ne, tm)
    rj = lax.broadcasted_iota(jnp.int32, (tm, tm), 0)
    cs = lax.broadcasted_iota(jnp.int32, (tm, tm), 1)
    before = jnp.where(rj < cs, 1.0, 0.0).astype(BF16)
    carry = carry_scr[...]
    rank2 = _dot(sel2.astype(BF16), before) + carry[:, 0:1]
    rank3 = rank2.reshape(shape3)
    new_carry = carry + jnp.sum(sel2, axis=1, keepdims=True)
    carry_scr[...] = new_carry
    cnt_ref[...] = new_carry.astype(jnp.int32)

    for k, (pick, ei) in enumerate(picks):
        red = lambda v: jnp.sum(jnp.sum(v, axis=1, keepdims=True), axis=0, keepdims=True).reshape(1, tm)
        idx_ref[k:k + 1, :] = ei.reshape(1, tm)
        rnk_ref[k:k + 1, :] = red(jnp.where(pick, rank3, 0.0)).astype(jnp.int32)
        gate_ref[k:k + 1, :] = red(jnp.where(pick, gates3, 0.0))


def _router_call(h2p, wr_t, bias):
    t = h2p.shape[0]
    ne, d = wr_t.shape
    tm = _pick(t, (512, 256))
    if t == tm:
        tm = tm // 2
    return pl.pallas_call(
        _router_kernel,
        out_shape=(jax.ShapeDtypeStruct((TOP_K, t), jnp.int32), jax.ShapeDtypeStruct((TOP_K, t), jnp.int32),
                   jax.ShapeDtypeStruct((TOP_K, t), F32), jax.ShapeDtypeStruct((ne, 128), jnp.int32)),
        grid=(t // tm,),
        in_specs=[
            pl.BlockSpec((tm, d // 2), lambda i: (i, 0)),
            pl.BlockSpec((ne, d), lambda i: (0, 0)),
            pl.BlockSpec((ne, 1), lambda i: (0, 0)),
        ],
        out_specs=(pl.BlockSpec((TOP_K, tm), lambda i: (0, i)), pl.BlockSpec((TOP_K, tm), lambda i: (0, i)),
                   pl.BlockSpec((TOP_K, tm), lambda i: (0, i)), pl.BlockSpec((ne, 128), lambda i: (0, 0))),
        scratch_shapes=[pltpu.VMEM((ne, 128), F32)],
        compiler_params=_params(("arbitrary",)),
        name="router",
    )(h2p, wr_t, bias.reshape(ne, 1))


def _dispatch_kernel(ps_ref, pc_ref, nu_ref, dest_ref, h_ref, meta_ref, xs_hbm, zbuf, rowbuf, idx_smem,
                     isem, ssem, zsem, tsem):
    tm = h_ref.shape[0]
    pw = h_ref.shape[1]
    ne = ps_ref.shape[0]
    tb = zbuf.shape[0]
    nblk = xs_hbm.shape[0] // tb

    @pl.when(pl.program_id(0) == 0)
    def _():
        zbuf[...] = jnp.zeros_like(zbuf)

        def tail(r, c):
            pltpu.make_async_copy(zbuf, xs_hbm.at[pl.ds(pl.multiple_of(r * tb, tb), tb)], tsem).start()
            return c

        def tail_drain(r, c):
            pltpu.make_async_copy(zbuf, xs_hbm.at[pl.ds(0, tb)], tsem).wait()
            return c

        lax.fori_loop(nu_ref[0], nblk, tail, 0)
        lax.fori_loop(nu_ref[0], nblk, tail_drain, 0)

        def fill(e, c):
            def one(j, c2):
                pltpu.make_async_copy(zbuf.at[0], xs_hbm.at[ps_ref[e] + j], zsem).start()
                return c2
            return lax.fori_loop(0, pc_ref[e], one, c)

        def drain(e, c):
            def one(j, c2):
                pltpu.make_async_copy(zbuf.at[0], xs_hbm.at[0], zsem).wait()
                return c2
            return lax.fori_loop(0, pc_ref[e], one, c)

        lax.fori_loop(0, ne, fill, 0)
        lax.fori_loop(0, ne, drain, 0)

    cp = pltpu.make_async_copy(dest_ref.at[0, 0], idx_smem, isem)
    cp.start()
    cp.wait()
    rowbuf[:, :pw] = h_ref[...]
    rowbuf[:, pw:] = meta_ref[...]

    def issue(i, c):
        for k in range(TOP_K):
            pltpu.make_async_copy(rowbuf.at[i], xs_hbm.at[idx_smem[k * tm + i]], ssem).start()
        return c

    lax.fori_loop(0, tm, issue, 0, unroll=2)
    pltpu.make_async_copy(xs_hbm.at[pl.ds(0, TOP_K * tm)], xs_hbm.at[pl.ds(0, TOP_K * tm)], ssem).wait()


def _dispatch_call(pad_start, pad_cnt, n_used, dest, h2p, meta, n_rows, tb):
    t, pw = h2p.shape
    d = pw + meta.shape[1]
    tm = _pick(t, (512, 256))
    nt = t // tm
    dest_t = dest.reshape(TOP_K, nt, tm).transpose(1, 0, 2).reshape(nt, 1, TOP_K * tm)
    grid_spec = pltpu.PrefetchScalarGridSpec(
        num_scalar_prefetch=3,
        grid=(nt,),
        in_specs=[
            pl.BlockSpec((1, 1, TOP_K * tm), lambda i, ps, pc, nu: (i, 0, 0)),
            pl.BlockSpec((tm, pw), lambda i, ps, pc, nu: (i, 0)),
            pl.BlockSpec((tm, meta.shape[1]), lambda i, ps, pc, nu: (i, 0)),
        ],
        out_specs=pl.BlockSpec(memory_space=pl.ANY),
        scratch_shapes=[pltpu.VMEM((tb, d), h2p.dtype), pltpu.VMEM((tm, d), h2p.dtype),
                        pltpu.SMEM((TOP_K * tm,), jnp.int32), pltpu.SemaphoreType.DMA,
                        pltpu.SemaphoreType.DMA, pltpu.SemaphoreType.DMA, pltpu.SemaphoreType.DMA],
    )
    return pl.pallas_call(
        _dispatch_kernel,
        out_shape=jax.ShapeDtypeStruct((n_rows, d), h2p.dtype),
        grid_spec=grid_spec,
        compiler_params=_params(("arbitrary",)),
        name="dispatch",
    )(pad_start, pad_cnt, n_used, dest_t, h2p, meta)


def _expert_kernel(be_ref, nu_ref, x_ref, wg_ref, wu_ref, wd_ref, y_hbm, wgb, wub, wdb, obuf, dstv, dst_smem,
                   isem, osem, *, tm, n_slots):
    r = pl.program_id(0)
    tb = x_ref.shape[0]
    pw = x_ref.shape[1] - META_W
    slot = r % 2
    prev = 1 - slot
    last = nu_ref[0] - 1

    def wait_rows(s):
        pltpu.make_async_copy(obuf.at[s], y_hbm.at[pl.ds(0, tb)], osem.at[s]).wait()

    def start_row(s, i):
        pltpu.make_async_copy(obuf.at[s, i], y_hbm.at[dst_smem[s, i]], osem.at[s]).start()

    @pl.when(r == 0)
    def _():
        obuf[...] = jnp.zeros_like(obuf)
        cp = pltpu.make_async_copy(obuf.at[0], y_hbm.at[pl.ds(n_slots, tb)], osem.at[0])
        cp.start()
        cp.wait()

        def init(i, c):
            dst_smem[1, i] = n_slots + tb + i
            return c

        lax.fori_loop(0, tb, init, 0)

    @pl.when(r <= last)
    def _():
        @pl.when(r >= 1)
        def _():
            wait_rows(slot)

        @pl.when(jnp.logical_or(r == 0, be_ref[r] != be_ref[jnp.maximum(r - 1, 0)]))
        def _():
            wgb[...] = wg_ref[0, 0].astype(BF16)
            wub[...] = wu_ref[0, 0].astype(BF16)
            wdb[...] = wd_ref[0, 0].astype(BF16)

        for i in range(tb):
            start_row(prev, i)

        xa, xb = _unpack_halves(x_ref[:, :pw])
        xa, xb = xa.astype(BF16), xb.astype(BF16)
        g = _dot(xa, wgb[:pw, :]) + _dot(xb, wgb[pw:, :])
        u = _dot(xa, wub[:pw, :]) + _dot(xb, wub[pw:, :])
        obuf[slot] = _pack_halves(_dot((_silu(g) * u).astype(BF16), wdb[...]))

        meta = lax.bitcast_convert_type(x_ref[:, pw:], jnp.int32)
        lane = lax.broadcasted_iota(jnp.int32, meta.shape, 1)
        hit = jnp.logical_and(meta == be_ref[r], jnp.logical_and(lane >= META_IDX, lane < META_IDX + TOP_K))
        kpos = jnp.sum(jnp.where(hit, lane - META_IDX, 0), axis=1, keepdims=True)
        tok = meta[:, META_TOK:META_TOK + 1]
        spare = n_slots + slot * tb + lax.broadcasted_iota(jnp.int32, (tb, 1), 0)
        home = (jnp.right_shift(tok, tm.bit_length() - 1) * TOP_K + kpos) * tm + jnp.bitwise_and(tok, tm - 1)
        dst = jnp.where(meta[:, 0:1] == 1, home, spare)
        dst_t = jnp.transpose(jnp.broadcast_to(dst.astype(F32), (tb, META_W)))
        dstv[...] = dst_t[0:8, :].astype(jnp.int32)
        cp = pltpu.make_async_copy(dstv.at[0], dst_smem.at[slot], isem)
        cp.start()
        cp.wait()

        @pl.when(r == last)
        def _():
            def issue(i, c):
                start_row(slot, i)
                return c

            lax.fori_loop(0, tb, issue, 0, unroll=8)
            wait_rows(prev)
            wait_rows(slot)


def _expert_call(block_expert, n_used, xs, w_gate, w_up, w_down, layer, tb, tm, n_tok):
    n_rows, xw = xs.shape
    pw = xw - META_W
    d, de = w_gate.shape[2:]
    nblk = n_rows // tb
    n_slots = n_tok * TOP_K
    assert tm & (tm - 1) == 0
    blk = lambda r, be, nu: jnp.minimum(r, nu[0] - 1)
    wmap = lambda r, be, nu: (layer, be[blk(r, be, nu)], 0, 0)
    grid_spec = pltpu.PrefetchScalarGridSpec(
        num_scalar_prefetch=2,
        grid=(nblk,),
        in_specs=[
            pl.BlockSpec((tb, xw), lambda r, be, nu: (blk(r, be, nu), 0)),
            pl.BlockSpec((1, 1, d, de), wmap),
            pl.BlockSpec((1, 1, d, de), wmap),
            pl.BlockSpec((1, 1, de, d), wmap),
        ],
        out_specs=pl.BlockSpec(memory_space=pl.ANY),
        scratch_shapes=[pltpu.VMEM((d, de), BF16), pltpu.VMEM((d, de), BF16), pltpu.VMEM((de, d), BF16),
                        pltpu.VMEM((2, tb, pw), jnp.uint32), pltpu.VMEM((8, tb), jnp.int32),
                        pltpu.SMEM((2, tb), jnp.int32), pltpu.SemaphoreType.DMA, pltpu.SemaphoreType.DMA((2,))],
    )
    return pl.pallas_call(
        functools.partial(_expert_kernel, tm=tm, n_slots=n_slots),
        out_shape=jax.ShapeDtypeStruct((n_slots + 2 * tb, pw), jnp.uint32),
        grid_spec=grid_spec,
        compiler_params=_params(("arbitrary",)),
        name="experts",
    )(block_expert, n_used, xs, w_gate, w_up, w_down)


def _combine_kernel(y_ref, gate_ref, h_ref, x_ref, mod_ref, sg_ref, su_ref, sd_ref, g_ref, b_ref, out_ref, *, alpha):
    tm = h_ref.shape[0]
    ha, hb = _unpack_halves(h_ref[...])
    ha, hb = ha.astype(BF16), hb.astype(BF16)
    half = ha.shape[1]
    sg = _dot(ha, sg_ref[:half, :]) + _dot(hb, sg_ref[half:, :])
    su = _dot(ha, su_ref[:half, :]) + _dot(hb, su_ref[half:, :])
    f = _dot((_silu(sg) * su).astype(BF16), sd_ref[...])
    fa, fb = f[:, :half], f[:, half:]
    gate = gate_ref[...]
    for k in range(TOP_K):
        oa, ob = _unpack_halves(y_ref[pl.ds(k * tm, tm), :])
        fa = fa + oa * gate[:, k:k + 1]
        fb = fb + ob * gate[:, k:k + 1]
    f = jnp.concatenate([fa, fb], axis=1)
    v = alpha * x_ref[...] + (1.0 + mod_ref[0, 5:6, :]) * f
    out_ref[...] = _ln_stats(v) * g_ref[...] + b_ref[...]


def _combine_call(y8, gate_t, h2p, x1, mod, sg, su, sd, ln_g, ln_b, seq, alpha, tm):
    t, d = x1.shape
    ds = sg.shape[1]
    per_b = seq // tm
    return pl.pallas_call(
        functools.partial(_combine_kernel, alpha=alpha),
        out_shape=jax.ShapeDtypeStruct((t, d), F32),
        grid=(t // tm,),
        in_specs=[
            pl.BlockSpec((TOP_K * tm, d // 2), lambda i: (i, 0)),
            pl.BlockSpec((tm, TOP_K), lambda i: (i, 0)),
            pl.BlockSpec((tm, d // 2), lambda i: (i, 0)),
            pl.BlockSpec((tm, d), lambda i: (i, 0)),
            pl.BlockSpec((1, 6, d), lambda i: (i // per_b, 0, 0)),
            pl.BlockSpec((d, ds), lambda i: (0, 0)),
            pl.BlockSpec((d, ds), lambda i: (0, 0)),
            pl.BlockSpec((ds, d), lambda i: (0, 0)),
            pl.BlockSpec((1, d), lambda i: (0, 0)),
            pl.BlockSpec((1, d), lambda i: (0, 0)),
        ],
        out_specs=pl.BlockSpec((tm, d), lambda i: (i, 0)),
        compiler_params=_params(("arbitrary",)),
        name="combine",
    )(y8, gate_t, h2p, x1, mod, sg, su, sd, ln_g.reshape(1, d), ln_b.reshape(1, d))


def _moe(h2, x1, mod, wr_t, router_bias, w_gate, w_up, w_down, layer, sg, su, sd, ln_g, ln_b, seq, alpha, tb):
    t, d = x1.shape
    ne = w_gate.shape[1]
    idx, rnk, gate, cnt = _router_call(h2, wr_t, router_bias)
    counts = cnt[:, 0]
    padded = (counts + tb - 1) // tb * tb
    pend = jnp.cumsum(padded)
    pstart = pend - padded
    eids = jnp.arange(ne, dtype=jnp.int32)
    dest = rnk + jnp.sum(jnp.where(idx[..., None] == eids, pstart.astype(jnp.int32), 0), axis=-1)
    nblk = (t * TOP_K) // tb + ne
    blk_row = jnp.arange(nblk, dtype=jnp.int32) * tb
    block_expert = jnp.minimum(jnp.sum((pend[None, :] <= blk_row[:, None]).astype(jnp.int32), axis=1), ne - 1)
    n_used = (pend[-1] // tb).astype(jnp.int32).reshape(1)
    zeros = lambda n: jnp.zeros((t, n), jnp.int32)
    meta = jnp.concatenate([jnp.ones((t, 1), jnp.int32), jnp.arange(t, dtype=jnp.int32)[:, None],
                            zeros(META_IDX - 2), idx.T, zeros(META_W - META_IDX - TOP_K)], axis=1)
    xs = _dispatch_call((pstart + counts).astype(jnp.int32), (padded - counts).astype(jnp.int32), n_used, dest, h2,
                        lax.bitcast_convert_type(meta, jnp.uint32), nblk * tb, tb)
    tm = _pick(seq, (256, 128))
    y8 = _expert_call(block_expert, n_used, xs, w_gate, w_up, w_down, layer, tb, tm, t)
    return _combine_call(y8, gate.T, h2, x1, mod, sg, su, sd, ln_g, ln_b, seq, alpha, tm)


def kernel(x, c, w_ada, b_ada, w_in, conv_w, conv_b, dt_bias, a_log, d_skip, sb_norm_w, ssd_norm_w, w_out, ln1_g,
           ln1_b, w_router, router_bias, w_gate, w_up, w_down, ws_gate, ws_up, ws_down, ln2_g, ln2_b):
    batch, seq, d = x.shape
    depth = w_ada.shape[0]
    sbw = sb_norm_w.shape[1]
    sb_heads = sbw // SB_HEAD_DIM
    ssd_heads = dt_bias.shape[1]
    n_main = w_in.shape[2] - ssd_heads
    alpha = (2 * depth) ** 0.25
    tb = 512 if (batch * seq * TOP_K) // w_gate.shape[1] >= 2048 else 128

    mod_all = _ada_call(c, w_ada, b_ada).reshape(depth, batch, 6, d)
    x2d = x.reshape(batch * seq, d)
    for l in range(depth):
        mod = mod_all[l]
        w_main = w_in[l, :, :n_main].astype(BF16)
        w_dt = jnp.pad(w_in[l, :, n_main:], ((0, 0), (0, 128 - ssd_heads))).astype(BF16)
        proj, dtp = _inproj_call(x2d, mod, w_main, w_dt, seq)
        o_sb = _sb_call(proj, sb_norm_w[l], batch, seq, sb_heads)
        o_ssd = _ssd_call(proj, dtp[:, :ssd_heads], conv_w[l], conv_b[l], dt_bias[l], a_log[l], d_skip[l],
                          ssd_norm_w[l], batch, seq, sbw)
        x1, h2 = _outproj_call(o_sb, o_ssd, x2d, mod, w_out[l].astype(BF16), ln1_g[l], ln1_b[l], seq, alpha)
        x2d = _moe(h2, x1, mod, w_router[l].T.astype(BF16), router_bias[l], w_gate, w_up, w_down, l,
                   ws_gate[l].astype(BF16), ws_up[l].astype(BF16), ws_down[l].astype(BF16), ln2_g[l], ln2_b[l],
                   seq, alpha, tb)
    return x2d.reshape(batch, seq, d)
```

```python
import functools
import math

import jax
import jax.numpy as jnp
from jax import lax
from jax.experimental import pallas as pl
from jax.experimental.pallas import tpu as pltpu

F32 = jnp.float32
BF16 = jnp.bfloat16

SB_HEAD_DIM = 128
SSD_HEAD_DIM = 64
SSD_GROUPS = 2
SSD_STATE = 128
SSD_CHUNK = 128
TOP_K = 8
N_EXPERT_GROUPS = 8
TOPK_GROUPS = 4
ROUTED_SCALE = 2.5
LN_EPS = 1e-5
RMS_EPS = 1e-6
SB_DEAD_LOG = -110.0
META_W = 128
META_TOK = 1
META_IDX = 8

VMEM_LIMIT_BYTES = 56 * 1024 * 1024


def _params(sem, vmem=VMEM_LIMIT_BYTES):
    return pltpu.CompilerParams(dimension_semantics=sem, vmem_limit_bytes=vmem)


def _pick(n, cands):
    for c in cands:
        if n % c == 0:
            return c
    raise ValueError(f"no tile in {cands} divides {n}")


def _softplus(z):
    return jnp.maximum(z, 0.0) + jnp.log(1.0 + jnp.exp(-jnp.abs(z)))


def _silu(v):
    return v * jax.nn.sigmoid(v)


def _split3(v):
    hi = v.astype(BF16)
    r1 = v - hi.astype(F32)
    mid = r1.astype(BF16)
    lo = (r1 - mid.astype(F32)).astype(BF16)
    return hi, mid, lo


def _dot(a, b):
    return jnp.dot(a, b, preferred_element_type=F32)


def _dot_nt(a, b):
    return lax.dot_general(a, b, (((1,), (1,)), ((), ())), preferred_element_type=F32)


def _dot_tn(a, b):
    return lax.dot_general(a, b, (((0,), (0,)), ((), ())), preferred_element_type=F32)


def _pack_halves(v):
    half = v.shape[1] // 2
    hi = lax.bitcast_convert_type(v[:, :half].astype(BF16).astype(F32), jnp.uint32)
    lo = lax.bitcast_convert_type(v[:, half:].astype(BF16).astype(F32), jnp.uint32)
    return hi | (lo >> 16)


def _unpack_halves(w):
    hi = lax.bitcast_convert_type(w & jnp.uint32(0xFFFF0000), F32)
    lo = lax.bitcast_convert_type(w << 16, F32)
    return hi, lo


def _ln_stats(v):
    mu = jnp.mean(v, axis=-1, keepdims=True)
    d = v - mu
    var = jnp.mean(d * d, axis=-1, keepdims=True)
    return d * lax.rsqrt(var + LN_EPS)


def _ada_kernel(c_ref, w_ref, b_ref, o_ref):
    ca = _silu(c_ref[...]).astype(BF16)
    o_ref[0] = _dot(ca, w_ref[0].astype(BF16)) + b_ref[0]


def _ada_call(c, w_ada, b_ada):
    depth, d, n = w_ada.shape
    b = c.shape[0]
    tn = _pick(n, (512, 256, 128))
    return pl.pallas_call(
        _ada_kernel,
        out_shape=jax.ShapeDtypeStruct((depth, b, n), F32),
        grid=(depth, n // tn),
        in_specs=[
            pl.BlockSpec((b, d), lambda l, j: (0, 0)),
            pl.BlockSpec((1, d, tn), lambda l, j: (l, 0, j)),
            pl.BlockSpec((1, 1, tn), lambda l, j: (l, 0, j)),
        ],
        out_specs=pl.BlockSpec((1, b, tn), lambda l, j: (l, 0, j)),
        compiler_params=_params(("arbitrary", "arbitrary")),
        name="ada_mod",
    )(c, w_ada, b_ada.reshape(depth, 1, n))


def _inproj_kernel(x_ref, mod_ref, w_ref, wdt_ref, o_ref, dt_ref, h_scr):
    @pl.when(pl.program_id(1) == 0)
    def _():
        xn = _ln_stats(x_ref[...])
        h = xn * (1.0 + mod_ref[0, 1:2, :]) + mod_ref[0, 0:1, :]
        hb = h.astype(BF16)
        h_scr[...] = hb
        dt_ref[...] = _dot(hb, wdt_ref[...])

    o_ref[...] = _dot(h_scr[...], w_ref[...]).astype(BF16)


def _inproj_call(x2d, mod, w_main, w_dt, seq):
    t, d = x2d.shape
    n = w_main.shape[1]
    tm = _pick(seq, (1024, 512, 256, 128))
    tn = _pick(n, (512, 256, 128))
    per_b = seq // tm
    return pl.pallas_call(
        _inproj_kernel,
        out_shape=(jax.ShapeDtypeStruct((t, n), BF16), jax.ShapeDtypeStruct((t, 128), F32)),
        grid=(t // tm, n // tn),
        in_specs=[
            pl.BlockSpec((tm, d), lambda i, j: (i, 0)),
            pl.BlockSpec((1, 6, d), lambda i, j: (i // per_b, 0, 0)),
            pl.BlockSpec((d, tn), lambda i, j: (0, j)),
            pl.BlockSpec((d, 128), lambda i, j: (0, 0)),
        ],
        out_specs=(
            pl.BlockSpec((tm, tn), lambda i, j: (i, j)),
            pl.BlockSpec((tm, 128), lambda i, j: (i, 0)),
        ),
        scratch_shapes=[pltpu.VMEM((tm, d), BF16)],
        compiler_params=_params(("arbitrary", "arbitrary")),
        name="in_proj",
    )(x2d, mod, w_main, w_dt)


def _sb_kernel(q_ref, k_ref, v_ref, nw_ref, o_ref, *, tq, tk, hg):
    qi = pl.program_id(2)
    hd = SB_HEAD_DIM
    scale = 1.0 / math.sqrt(hd)
    nd = tq // tk
    rj = lax.broadcasted_iota(jnp.int32, (tq, tk), 0)
    cs = lax.broadcasted_iota(jnp.int32, (tq, tk), 1)
    lr = lax.broadcasted_iota(jnp.int32, (tk, tk), 0)
    lc = lax.broadcasted_iota(jnp.int32, (tk, tk), 1)
    later = jnp.where(lr > lc, 1.0, 0.0).astype(BF16)
    qs = [q_ref[:, g * hd:(g + 1) * hd] for g in range(hg)]

    def block(kb, accs, runs, offset):
        start = pl.multiple_of(kb * tk, tk)
        new_accs, new_runs = [], []
        for g in range(hg):
            k = k_ref[pl.ds(start, tk), g * hd:(g + 1) * hd]
            v = v_ref[pl.ds(start, tk), g * hd:(g + 1) * hd]
            z = _dot_nt(qs[g], k) * scale
            sp = _softplus(z)
            if offset is None:
                lom = -sp
            else:
                mask = (cs + offset) < rj
                lom = jnp.where(mask, -sp, 0.0)
            hi = lom.astype(BF16)
            lo = (lom - hi.astype(F32)).astype(BF16)
            between = _dot(hi, later) + _dot(lo, later) + runs[g]
            w = jnp.exp(z - sp + between)
            if offset is not None:
                w = jnp.where(mask, w, 0.0)
            new_accs.append(accs[g] + _dot(w.astype(BF16), v))
            new_runs.append(runs[g] + jnp.sum(lom, axis=1, keepdims=True))
        return new_accs, new_runs

    accs = [jnp.zeros((tq, hd), F32) for _ in range(hg)]
    runs = [jnp.zeros((tq, 1), F32) for _ in range(hg)]
    for j in range(nd):
        accs, runs = block(qi * nd + (nd - 1 - j), accs, runs, (nd - 1 - j) * tk)

    def run_max(rs):
        m = rs[0]
        for r in rs[1:]:
            m = jnp.maximum(m, r)
        return jnp.max(m)

    def cond(carry):
        i, top, _, _ = carry
        return jnp.logical_and(i < qi * nd, top > SB_DEAD_LOG)

    def body(carry):
        i, _, accs, runs = carry
        accs, runs = block(qi * nd - 1 - i, list(accs), list(runs), None)
        return i + 1, run_max(runs), tuple(accs), tuple(runs)

    _, _, accs, _ = lax.while_loop(cond, body, (jnp.int32(0), run_max(runs), tuple(accs), tuple(runs)))
    for g in range(hg):
        acc = accs[g]
        o = acc * lax.rsqrt(jnp.mean(acc * acc, axis=-1, keepdims=True) + RMS_EPS) * nw_ref[:, g * hd:(g + 1) * hd]
        o_ref[:, g * hd:(g + 1) * hd] = o.astype(BF16)


def _sb_call(proj, sb_norm_w, batch, seq, heads):
    t = proj.shape[0]
    tq = _pick(seq, (256, 128))
    tk = tq
    hg = 4 if heads % 4 == 0 else 2
    nq = seq // tq
    hb = heads // hg
    wd = hg * SB_HEAD_DIM
    return pl.pallas_call(
        functools.partial(_sb_kernel, tq=tq, tk=tk, hg=hg),
        out_shape=jax.ShapeDtypeStruct((t, heads * SB_HEAD_DIM), BF16),
        grid=(batch, hb, nq),
        in_specs=[
            pl.BlockSpec((tq, wd), lambda b, h, i: (b * nq + i, h)),
            pl.BlockSpec((seq, wd), lambda b, h, i: (b, hb + h)),
            pl.BlockSpec((seq, wd), lambda b, h, i: (b, 2 * hb + h)),
            pl.BlockSpec((1, wd), lambda b, h, i: (0, h)),
        ],
        out_specs=pl.BlockSpec((tq, wd), lambda b, h, i: (b * nq + i, h)),
        compiler_params=_params(("arbitrary", "arbitrary", "arbitrary")),
        name="sb_attn",
    )(proj, proj, proj, sb_norm_w.reshape(1, -1))


def _ssd_kernel(z_ref, xs_ref, b_ref, c_ref, dt_ref, dtt_ref, cw_ref, cb_ref, dtb_ref, dtbt_ref,
                alog_ref, alogt_ref, dskip_ref, nw_ref, o_ref, prev_scr, state_scr, *, heads):
    ln = xs_ref.shape[0]
    w = xs_ref.shape[1]
    gn = b_ref.shape[1]
    n = gn // SSD_GROUPS
    wg = w // SSD_GROUPS
    hpg = heads // SSD_GROUPS

    @pl.when(pl.program_id(1) == 0)
    def _():
        prev_scr[...] = jnp.zeros_like(prev_scr)
        state_scr[...] = jnp.zeros_like(state_scr)

    raw = jnp.concatenate([xs_ref[...], b_ref[...], c_ref[...]], axis=1).astype(F32)
    pr = prev_scr[...]
    ch = raw.shape[1]
    row = lax.broadcasted_iota(jnp.int32, (ln, ch), 0)
    taps = cw_ref.shape[0]
    acc = raw * cw_ref[taps - 1:taps, :] + cb_ref[...]
    for sh in range(1, taps):
        shifted = jnp.where(row < sh, pltpu.roll(pr, sh, 0), pltpu.roll(raw, sh, 0))
        acc = acc + shifted * cw_ref[taps - 1 - sh:taps - sh, :]
    prev_scr[...] = raw
    xbc = _silu(acc)
    xs = xbc[:, :w]
    bm = xbc[:, w:w + gn]
    cm = xbc[:, w + gn:]

    dtv = _softplus(dt_ref[0] + dtb_ref[...])
    dtvt = _softplus(dtt_ref[0] + dtbt_ref[...])
    a_full = -jnp.exp(alog_ref[...])
    a_t = -jnp.exp(alogt_ref[...])

    eh = lax.broadcasted_iota(jnp.int32, (heads, w), 0)
    ec = lax.broadcasted_iota(jnp.int32, (heads, w), 1)
    expand = jnp.where(ec // SSD_HEAD_DIM == eh, 1.0, 0.0).astype(BF16)
    d_hi, d_mid, d_lo = _split3(dtv)
    dt_exp = _dot(d_hi, expand) + _dot(d_mid, expand) + _dot(d_lo, expand)

    rl = lax.broadcasted_iota(jnp.int32, (ln, ln), 0)
    cl = lax.broadcasted_iota(jnp.int32, (ln, ln), 1)
    tril = cl <= rl
    incl = jnp.where(tril, 1.0, 0.0).astype(BF16)
    incl_t = jnp.where(rl <= cl, 1.0, 0.0).astype(BF16)

    a_exp = dt_exp * a_full
    a_hi, a_mid, a_lo = _split3(a_exp)
    acum = _dot(incl, a_hi) + _dot(incl, a_mid) + _dot(incl, a_lo)
    at = dtvt * a_t
    t_hi, t_mid, t_lo = _split3(at)
    acum_t = _dot(t_hi, incl_t) + _dot(t_mid, incl_t) + _dot(t_lo, incl_t)

    xdt = xs * dt_exp
    last = acum[ln - 1:ln, :]
    xdd = (xdt * jnp.exp(last - acum)).astype(BF16)
    xdtb = xdt.astype(BF16)
    ea = jnp.exp(acum)
    cdec = jnp.exp(last)

    col_head = lax.broadcasted_iota(jnp.int32, (ln, wg), 1) // SSD_HEAD_DIM
    ys = []
    for g in range(SSD_GROUPS):
        bg = bm[:, g * n:(g + 1) * n].astype(BF16)
        cg = cm[:, g * n:(g + 1) * n].astype(BF16)
        cb = _dot_nt(cg, bg)
        ms = []
        for e in range(hpg):
            h = g * hpg + e
            ac = acum[:, h * SSD_HEAD_DIM:h * SSD_HEAD_DIM + 1]
            ar = acum_t[h:h + 1, :]
            lmat = jnp.exp(jnp.where(tril, ac - ar, -jnp.inf))
            ms.append((cb * lmat).astype(BF16))
        mcat = jnp.concatenate(ms, axis=1)
        xg = xdtb[:, g * wg:(g + 1) * wg]
        xbd = jnp.concatenate([jnp.where(col_head == e, xg, jnp.zeros_like(xg)) for e in range(hpg)], axis=0)
        y_diag = _dot(mcat, xbd)
        sg = state_scr[g]
        y_off = _dot(cg, sg.astype(BF16)) * ea[:, g * wg:(g + 1) * wg]
        s_new = _dot_tn(bg, xdd[:, g * wg:(g + 1) * wg])
        state_scr[g] = sg * cdec[:, g * wg:(g + 1) * wg] + s_new
        ys.append(y_diag + y_off)
    y = jnp.concatenate(ys, axis=1) + xs * dskip_ref[...]
    y = y * _silu(z_ref[...].astype(F32))
    o = y * lax.rsqrt(jnp.mean(y * y, axis=-1, keepdims=True) + RMS_EPS) * nw_ref[...]
    o_ref[...] = o.astype(BF16)


def _ssd_call(proj, dt, conv_w, conv_b, dt_bias, a_log, d_skip, ssd_norm_w, batch, seq, sbw):
    t = proj.shape[0]
    heads = dt_bias.shape[0]
    w = heads * SSD_HEAD_DIM
    gn = SSD_GROUPS * SSD_STATE
    ch = w + 2 * gn
    ln = SSD_CHUNK
    nc = seq // ln
    dt3 = dt.reshape(batch, seq, heads)
    dtt = jnp.swapaxes(dt3, 1, 2)
    rep = lambda v: jnp.repeat(v, SSD_HEAD_DIM).reshape(1, w)
    z_blk, xs_blk = 3 * sbw // w, (3 * sbw + w) // w
    b_blk, c_blk = (3 * sbw + 2 * w) // gn, (3 * sbw + 2 * w + gn) // gn
    assert 3 * sbw % w == 0 and (3 * sbw + 2 * w) % gn == 0
    full = lambda shape: pl.BlockSpec(shape, lambda b, c: (0,) * len(shape))
    return pl.pallas_call(
        functools.partial(_ssd_kernel, heads=heads),
        out_shape=jax.ShapeDtypeStruct((t, w), BF16),
        grid=(batch, nc),
        in_specs=[
            pl.BlockSpec((ln, w), lambda b, c: (b * nc + c, z_blk)),
            pl.BlockSpec((ln, w), lambda b, c: (b * nc + c, xs_blk)),
            pl.BlockSpec((ln, gn), lambda b, c: (b * nc + c, b_blk)),
            pl.BlockSpec((ln, gn), lambda b, c: (b * nc + c, c_blk)),
            pl.BlockSpec((1, ln, heads), lambda b, c: (b, c, 0)),
            pl.BlockSpec((1, heads, ln), lambda b, c: (b, 0, c)),
            full(conv_w.shape), full((1, ch)), full((1, heads)), full((heads, 1)),
            full((1, w)), full((heads, 1)), full((1, w)), full((1, w)),
        ],
        out_specs=pl.BlockSpec((ln, w), lambda b, c: (b * nc + c, 0)),
        scratch_shapes=[pltpu.VMEM((ln, ch), F32), pltpu.VMEM((SSD_GROUPS, SSD_STATE, w // SSD_GROUPS), F32)],
        compiler_params=_params(("arbitrary", "arbitrary")),
        name="ssd",
    )(proj, proj, proj, proj, dt3, dtt, conv_w, conv_b.reshape(1, ch), dt_bias.reshape(1, heads),
      dt_bias.reshape(heads, 1), rep(a_log), a_log.reshape(heads, 1), rep(d_skip), ssd_norm_w.reshape(1, w))


def _outproj_kernel(osb_ref, ossd_ref, x_ref, mod_ref, w1_ref, w2_ref, g_ref, b_ref, x1_ref, h2_ref, *, alpha):
    m = _dot(osb_ref[...], w1_ref[...]) + _dot(ossd_ref[...], w2_ref[...])
    v = alpha * x_ref[...] + (1.0 + mod_ref[0, 2:3, :]) * m
    x1 = _ln_stats(v) * g_ref[...] + b_ref[...]
    x1_ref[...] = x1
    h2_ref[...] = _pack_halves(_ln_stats(x1) * (1.0 + mod_ref[0, 4:5, :]) + mod_ref[0, 3:4, :])


def _outproj_call(o_sb, o_ssd, x2d, mod, w_out_b, ln_g, ln_b, seq, alpha):
    t, d = x2d.shape
    sbw, ssw = o_sb.shape[1], o_ssd.shape[1]
    tm = _pick(seq, (256, 128))
    per_b = seq // tm
    return pl.pallas_call(
        functools.partial(_outproj_kernel, alpha=alpha),
        out_shape=(jax.ShapeDtypeStruct((t, d), F32), jax.ShapeDtypeStruct((t, d // 2), jnp.uint32)),
        grid=(t // tm,),
        in_specs=[
            pl.BlockSpec((tm, sbw), lambda i: (i, 0)),
            pl.BlockSpec((tm, ssw), lambda i: (i, 0)),
            pl.BlockSpec((tm, d), lambda i: (i, 0)),
            pl.BlockSpec((1, 6, d), lambda i: (i // per_b, 0, 0)),
            pl.BlockSpec((sbw, d), lambda i: (0, 0)),
            pl.BlockSpec((ssw, d), lambda i: (0, 0)),
            pl.BlockSpec((1, d), lambda i: (0, 0)),
            pl.BlockSpec((1, d), lambda i: (0, 0)),
        ],
        out_specs=(pl.BlockSpec((tm, d), lambda i: (i, 0)), pl.BlockSpec((tm, d // 2), lambda i: (i, 0))),
        compiler_params=_params(("arbitrary",)),
        name="out_proj",
    )(o_sb, o_ssd, x2d, mod, w_out_b[:sbw], w_out_b[sbw:], ln_g.reshape(1, d), ln_b.reshape(1, d))


def _router_kernel(h_ref, wr_ref, bias_ref, idx_ref, rnk_ref, gate_ref, cnt_ref, carry_scr):
    ne = wr_ref.shape[0]
    tm = h_ref.shape[0]
    per_g = ne // N_EXPERT_GROUPS
    neg = -jnp.inf

    @pl.when(pl.program_id(0) == 0)
    def _():
        carry_scr[...] = jnp.zeros_like(carry_scr)

    ha, hb = _unpack_halves(h_ref[...])
    half = ha.shape[1]
    logits = _dot_nt(wr_ref[:, :half], ha.astype(BF16)) + _dot_nt(wr_ref[:, half:], hb.astype(BF16))
    scores = jax.nn.sigmoid(logits)
    biased = scores + bias_ref[...]
    s3 = scores.reshape(N_EXPERT_GROUPS, per_g, tm)
    b3 = biased.reshape(N_EXPERT_GROUPS, per_g, tm)
    shape3 = (N_EXPERT_GROUPS, per_g, tm)
    j_io = lax.broadcasted_iota(jnp.int32, shape3, 1)
    g_io = lax.broadcasted_iota(jnp.int32, shape3, 0)
    e_io = g_io * per_g + j_io

    m1 = jnp.max(b3, axis=1, keepdims=True)
    i1 = jnp.min(jnp.where(b3 == m1, j_io, per_g), axis=1, keepdims=True)
    m2 = jnp.max(jnp.where(j_io == i1, neg, b3), axis=1, keepdims=True)
    gs = m1 + m2
    gg = lax.broadcasted_iota(jnp.int32, gs.shape, 0)
    gmask = jnp.zeros(gs.shape, F32)
    cur = gs
    for _ in range(TOPK_GROUPS):
        m = jnp.max(cur, axis=0, keepdims=True)
        gi = jnp.min(jnp.where(cur == m, gg, N_EXPERT_GROUPS), axis=0, keepdims=True)
        pick = gg == gi
        gmask = jnp.where(pick, 1.0, gmask)
        cur = jnp.where(pick, neg, cur)
    cur = jnp.where(jnp.broadcast_to(gmask, shape3) > 0.5, b3, neg)

    sel = jnp.zeros(shape3, F32)
    picks = []
    for _ in range(TOP_K):
        m = jnp.max(jnp.max(cur, axis=1, keepdims=True), axis=0, keepdims=True)
        ei = jnp.min(jnp.min(jnp.where(cur == m, e_io, ne), axis=1, keepdims=True), axis=0, keepdims=True)
        pick = e_io == ei
        sel = jnp.where(pick, 1.0, sel)
        cur = jnp.where(pick, neg, cur)
        picks.append((pick, ei))

    selw = sel * s3
    denom = jnp.sum(jnp.sum(selw, axis=1, keepdims=True), axis=0, keepdims=True)
    gates3 = selw / denom * ROUTED_SCALE

    sel2 = sel.reshape(ne, tm)
    rj = lax.broadcasted_iota(jnp.int32, (tm, tm), 0)
    cs = lax.broadcasted_iota(jnp.int32, (tm, tm), 1)
    before = jnp.where(rj < cs, 1.0, 0.0).astype(BF16)
    carry = carry_scr[...]
    rank2 = _dot(sel2.astype(BF16), before) + carry[:, 0:1]
    rank3 = rank2.reshape(shape3)
    new_carry = carry + jnp.sum(sel2, axis=1, keepdims=True)
    carry_scr[...] = new_carry
    cnt_ref[...] = new_carry.astype(jnp.int32)

    for k, (pick, ei) in enumerate(picks):
        red = lambda v: jnp.sum(jnp.sum(v, axis=1, keepdims=True), axis=0, keepdims=True).reshape(1, tm)
        idx_ref[k:k + 1, :] = ei.reshape(1, tm)
        rnk_ref[k:k + 1, :] = red(jnp.where(pick, rank3, 0.0)).astype(jnp.int32)
        gate_ref[k:k + 1, :] = red(jnp.where(pick, gates3, 0.0))


def _router_call(h2p, wr_t, bias):
    t = h2p.shape[0]
    ne, d = wr_t.shape
    tm = _pick(t, (512, 256))
    if t == tm:
        tm = tm // 2
    return pl.pallas_call(
        _router_kernel,
        out_shape=(jax.ShapeDtypeStruct((TOP_K, t), jnp.int32), jax.ShapeDtypeStruct((TOP_K, t), jnp.int32),
                   jax.ShapeDtypeStruct((TOP_K, t), F32), jax.ShapeDtypeStruct((ne, 128), jnp.int32)),
        grid=(t // tm,),
        in_specs=[
            pl.BlockSpec((tm, d // 2), lambda i: (i, 0)),
            pl.BlockSpec((ne, d), lambda i: (0, 0)),
            pl.BlockSpec((ne, 1), lambda i: (0, 0)),
        ],
        out_specs=(pl.BlockSpec((TOP_K, tm), lambda i: (0, i)), pl.BlockSpec((TOP_K, tm), lambda i: (0, i)),
                   pl.BlockSpec((TOP_K, tm), lambda i: (0, i)), pl.BlockSpec((ne, 128), lambda i: (0, 0))),
        scratch_shapes=[pltpu.VMEM((ne, 128), F32)],
        compiler_params=_params(("arbitrary",)),
        name="router",
    )(h2p, wr_t, bias.reshape(ne, 1))


def _dispatch_kernel(ps_ref, pc_ref, nu_ref, dest_ref, h_ref, meta_ref, xs_hbm, zbuf, rowbuf, idx_smem,
                     isem, ssem, zsem, tsem):
    tm = h_ref.shape[0]
    pw = h_ref.shape[1]
    ne = ps_ref.shape[0]
    tb = zbuf.shape[0]
    nblk = xs_hbm.shape[0] // tb

    @pl.when(pl.program_id(0) == 0)
    def _():
        zbuf[...] = jnp.zeros_like(zbuf)

        def tail(r, c):
            pltpu.make_async_copy(zbuf, xs_hbm.at[pl.ds(pl.multiple_of(r * tb, tb), tb)], tsem).start()
            return c

        def tail_drain(r, c):
            pltpu.make_async_copy(zbuf, xs_hbm.at[pl.ds(0, tb)], tsem).wait()
            return c

        lax.fori_loop(nu_ref[0], nblk, tail, 0)
        lax.fori_loop(nu_ref[0], nblk, tail_drain, 0)

        def fill(e, c):
            def one(j, c2):
                pltpu.make_async_copy(zbuf.at[0], xs_hbm.at[ps_ref[e] + j], zsem).start()
                return c2
            return lax.fori_loop(0, pc_ref[e], one, c)

        def drain(e, c):
            def one(j, c2):
                pltpu.make_async_copy(zbuf.at[0], xs_hbm.at[0], zsem).wait()
                return c2
            return lax.fori_loop(0, pc_ref[e], one, c)

        lax.fori_loop(0, ne, fill, 0)
        lax.fori_loop(0, ne, drain, 0)

    cp = pltpu.make_async_copy(dest_ref.at[0, 0], idx_smem, isem)
    cp.start()
    cp.wait()
    rowbuf[:, :pw] = h_ref[...]
    rowbuf[:, pw:] = meta_ref[...]

    def issue(i, c):
        for k in range(TOP_K):
            pltpu.make_async_copy(rowbuf.at[i], xs_hbm.at[idx_smem[k * tm + i]], ssem).start()
        return c

    lax.fori_loop(0, tm, issue, 0, unroll=2)
    pltpu.make_async_copy(xs_hbm.at[pl.ds(0, TOP_K * tm)], xs_hbm.at[pl.ds(0, TOP_K * tm)], ssem).wait()


def _dispatch_call(pad_start, pad_cnt, n_used, dest, h2p, meta, n_rows, tb):
    t, pw = h2p.shape
    d = pw + meta.shape[1]
    tm = _pick(t, (512, 256))
    nt = t // tm
    dest_t = dest.reshape(TOP_K, nt, tm).transpose(1, 0, 2).reshape(nt, 1, TOP_K * tm)
    grid_spec = pltpu.PrefetchScalarGridSpec(
        num_scalar_prefetch=3,
        grid=(nt,),
        in_specs=[
            pl.BlockSpec((1, 1, TOP_K * tm), lambda i, ps, pc, nu: (i, 0, 0)),
            pl.BlockSpec((tm, pw), lambda i, ps, pc, nu: (i, 0)),
            pl.BlockSpec((tm, meta.shape[1]), lambda i, ps, pc, nu: (i, 0)),
        ],
        out_specs=pl.BlockSpec(memory_space=pl.ANY),
        scratch_shapes=[pltpu.VMEM((tb, d), h2p.dtype), pltpu.VMEM((tm, d), h2p.dtype),
                        pltpu.SMEM((TOP_K * tm,), jnp.int32), pltpu.SemaphoreType.DMA,
                        pltpu.SemaphoreType.DMA, pltpu.SemaphoreType.DMA, pltpu.SemaphoreType.DMA],
    )
    return pl.pallas_call(
        _dispatch_kernel,
        out_shape=jax.ShapeDtypeStruct((n_rows, d), h2p.dtype),
        grid_spec=grid_spec,
        compiler_params=_params(("arbitrary",)),
        name="dispatch",
    )(pad_start, pad_cnt, n_used, dest_t, h2p, meta)


def _expert_kernel(be_ref, nu_ref, x_ref, wg_ref, wu_ref, wd_ref, y_hbm, wgb, wub, wdb, obuf, dstv, dst_smem,
                   isem, osem, *, tm, n_slots):
    r = pl.program_id(0)
    tb = x_ref.shape[0]
    pw = x_ref.shape[1] - META_W
    slot = r % 2
    prev = 1 - slot
    last = nu_ref[0] - 1

    def wait_rows(s):
        pltpu.make_async_copy(obuf.at[s], y_hbm.at[pl.ds(0, tb)], osem.at[s]).wait()

    def start_row(s, i):
        pltpu.make_async_copy(obuf.at[s, i], y_hbm.at[dst_smem[s, i]], osem.at[s]).start()

    @pl.when(r == 0)
    def _():
        obuf[...] = jnp.zeros_like(obuf)
        cp = pltpu.make_async_copy(obuf.at[0], y_hbm.at[pl.ds(n_slots, tb)], osem.at[0])
        cp.start()
        cp.wait()

        def init(i, c):
            dst_smem[1, i] = n_slots + tb + i
            return c

        lax.fori_loop(0, tb, init, 0)

    @pl.when(r <= last)
    def _():
        @pl.when(r >= 1)
        def _():
            wait_rows(slot)

        @pl.when(jnp.logical_or(r == 0, be_ref[r] != be_ref[jnp.maximum(r - 1, 0)]))
        def _():
            wgb[...] = wg_ref[0, 0].astype(BF16)
            wub[...] = wu_ref[0, 0].astype(BF16)
            wdb[...] = wd_ref[0, 0].astype(BF16)

        quarter = tb // 4

        def send_group(q):
            for i in range(q * quarter, (q + 1) * quarter):
                start_row(prev, i)

        def mark(q, v):
            dst_smem[slot, tb + q] = jnp.max(v[0:8, 0:128]).astype(jnp.int32)

        xa, xb = _unpack_halves(x_ref[:, :pw])
        xa, xb = xa.astype(BF16), xb.astype(BF16)
        send_group(0)
        g = _dot(xa, wgb[:pw, :]) + _dot(xb, wgb[pw:, :])
        mark(0, g)
        send_group(1)
        u = _dot(xa, wub[:pw, :]) + _dot(xb, wub[pw:, :])
        mark(1, u)
        send_group(2)
        a = (_silu(g) * u).astype(BF16)
        o_left = _dot(a, wdb[:, :pw])
        mark(2, o_left)
        send_group(3)
        o_right = _dot(a, wdb[:, pw:])
        obuf[slot] = _pack_halves(jnp.concatenate([o_left, o_right], axis=1))

        meta = lax.bitcast_convert_type(x_ref[:, pw:], jnp.int32)
        lane = lax.broadcasted_iota(jnp.int32, meta.shape, 1)
        hit = jnp.logical_and(meta == be_ref[r], jnp.logical_and(lane >= META_IDX, lane < META_IDX + TOP_K))
        kpos = jnp.sum(jnp.where(hit, lane - META_IDX, 0), axis=1, keepdims=True)
        tok = meta[:, META_TOK:META_TOK + 1]
        spare = n_slots + slot * tb + lax.broadcasted_iota(jnp.int32, (tb, 1), 0)
        home = (jnp.right_shift(tok, tm.bit_length() - 1) * TOP_K + kpos) * tm + jnp.bitwise_and(tok, tm - 1)
        dst = jnp.where(meta[:, 0:1] == 1, home, spare)
        dst_t = jnp.transpose(jnp.broadcast_to(dst.astype(F32), (tb, META_W)))
        dstv[...] = dst_t[0:8, :].astype(jnp.int32)
        cp = pltpu.make_async_copy(dstv.at[0], dst_smem.at[slot, pl.ds(0, tb)], isem)
        cp.start()
        cp.wait()

        @pl.when(r == last)
        def _():
            def issue(i, c):
                start_row(slot, i)
                return c

            lax.fori_loop(0, tb, issue, 0, unroll=8)
            wait_rows(prev)
            wait_rows(slot)


def _expert_call(block_expert, n_used, xs, w_gate, w_up, w_down, layer, tb, tm, n_tok):
    n_rows, xw = xs.shape
    pw = xw - META_W
    d, de = w_gate.shape[2:]
    nblk = n_rows // tb
    n_slots = n_tok * TOP_K
    assert tm & (tm - 1) == 0
    blk = lambda r, be, nu: jnp.minimum(r, nu[0] - 1)
    wmap = lambda r, be, nu: (layer, be[blk(r, be, nu)], 0, 0)
    grid_spec = pltpu.PrefetchScalarGridSpec(
        num_scalar_prefetch=2,
        grid=(nblk,),
        in_specs=[
            pl.BlockSpec((tb, xw), lambda r, be, nu: (blk(r, be, nu), 0)),
            pl.BlockSpec((1, 1, d, de), wmap),
            pl.BlockSpec((1, 1, d, de), wmap),
            pl.BlockSpec((1, 1, de, d), wmap),
        ],
        out_specs=pl.BlockSpec(memory_space=pl.ANY),
        scratch_shapes=[pltpu.VMEM((d, de), BF16), pltpu.VMEM((d, de), BF16), pltpu.VMEM((de, d), BF16),
                        pltpu.VMEM((2, tb, pw), jnp.uint32), pltpu.VMEM((8, tb), jnp.int32),
                        pltpu.SMEM((2, tb + 128), jnp.int32), pltpu.SemaphoreType.DMA, pltpu.SemaphoreType.DMA((2,))],
    )
    return pl.pallas_call(
        functools.partial(_expert_kernel, tm=tm, n_slots=n_slots),
        out_shape=jax.ShapeDtypeStruct((n_slots + 2 * tb, pw), jnp.uint32),
        grid_spec=grid_spec,
        compiler_params=_params(("arbitrary",)),
        name="experts",
    )(block_expert, n_used, xs, w_gate, w_up, w_down)


def _combine_kernel(y_ref, gate_ref, h_ref, x_ref, mod_ref, sg_ref, su_ref, sd_ref, g_ref, b_ref, out_ref, *, alpha):
    tm = h_ref.shape[0]
    ha, hb = _unpack_halves(h_ref[...])
    ha, hb = ha.astype(BF16), hb.astype(BF16)
    half = ha.shape[1]
    sg = _dot(ha, sg_ref[:half, :]) + _dot(hb, sg_ref[half:, :])
    su = _dot(ha, su_ref[:half, :]) + _dot(hb, su_ref[half:, :])
    f = _dot((_silu(sg) * su).astype(BF16), sd_ref[...])
    fa, fb = f[:, :half], f[:, half:]
    gate = gate_ref[...]
    for k in range(TOP_K):
        oa, ob = _unpack_halves(y_ref[pl.ds(k * tm, tm), :])
        fa = fa + oa * gate[:, k:k + 1]
        fb = fb + ob * gate[:, k:k + 1]
    f = jnp.concatenate([fa, fb], axis=1)
    v = alpha * x_ref[...] + (1.0 + mod_ref[0, 5:6, :]) * f
    out_ref[...] = _ln_stats(v) * g_ref[...] + b_ref[...]


def _combine_call(y8, gate_t, h2p, x1, mod, sg, su, sd, ln_g, ln_b, seq, alpha, tm):
    t, d = x1.shape
    ds = sg.shape[1]
    per_b = seq // tm
    return pl.pallas_call(
        functools.partial(_combine_kernel, alpha=alpha),
        out_shape=jax.ShapeDtypeStruct((t, d), F32),
        grid=(t // tm,),
        in_specs=[
            pl.BlockSpec((TOP_K * tm, d // 2), lambda i: (i, 0)),
            pl.BlockSpec((tm, TOP_K), lambda i: (i, 0)),
            pl.BlockSpec((tm, d // 2), lambda i: (i, 0)),
            pl.BlockSpec((tm, d), lambda i: (i, 0)),
            pl.BlockSpec((1, 6, d), lambda i: (i // per_b, 0, 0)),
            pl.BlockSpec((d, ds), lambda i: (0, 0)),
            pl.BlockSpec((d, ds), lambda i: (0, 0)),
            pl.BlockSpec((ds, d), lambda i: (0, 0)),
            pl.BlockSpec((1, d), lambda i: (0, 0)),
            pl.BlockSpec((1, d), lambda i: (0, 0)),
        ],
        out_specs=pl.BlockSpec((tm, d), lambda i: (i, 0)),
        compiler_params=_params(("arbitrary",)),
        name="combine",
    )(y8, gate_t, h2p, x1, mod, sg, su, sd, ln_g.reshape(1, d), ln_b.reshape(1, d))


def _moe(h2, x1, mod, wr_t, router_bias, w_gate, w_up, w_down, layer, sg, su, sd, ln_g, ln_b, seq, alpha, tb):
    t, d = x1.shape
    ne = w_gate.shape[1]
    idx, rnk, gate, cnt = _router_call(h2, wr_t, router_bias)
    counts = cnt[:, 0]
    padded = (counts + tb - 1) // tb * tb
    pend = jnp.cumsum(padded)
    pstart = pend - padded
    eids = jnp.arange(ne, dtype=jnp.int32)
    dest = rnk + jnp.sum(jnp.where(idx[..., None] == eids, pstart.astype(jnp.int32), 0), axis=-1)
    nblk = (t * TOP_K) // tb + ne
    blk_row = jnp.arange(nblk, dtype=jnp.int32) * tb
    block_expert = jnp.minimum(jnp.sum((pend[None, :] <= blk_row[:, None]).astype(jnp.int32), axis=1), ne - 1)
    n_used = (pend[-1] // tb).astype(jnp.int32).reshape(1)
    zeros = lambda n: jnp.zeros((t, n), jnp.int32)
    meta = jnp.concatenate([jnp.ones((t, 1), jnp.int32), jnp.arange(t, dtype=jnp.int32)[:, None],
                            zeros(META_IDX - 2), idx.T, zeros(META_W - META_IDX - TOP_K)], axis=1)
    xs = _dispatch_call((pstart + counts).astype(jnp.int32), (padded - counts).astype(jnp.int32), n_used, dest, h2,
                        lax.bitcast_convert_type(meta, jnp.uint32), nblk * tb, tb)
    tm = _pick(seq, (256, 128))
    y8 = _expert_call(block_expert, n_used, xs, w_gate, w_up, w_down, layer, tb, tm, t)
    return _combine_call(y8, gate.T, h2, x1, mod, sg, su, sd, ln_g, ln_b, seq, alpha, tm)


def kernel(x, c, w_ada, b_ada, w_in, conv_w, conv_b, dt_bias, a_log, d_skip, sb_norm_w, ssd_norm_w, w_out, ln1_g,
           ln1_b, w_router, router_bias, w_gate, w_up, w_down, ws_gate, ws_up, ws_down, ln2_g, ln2_b):
    batch, seq, d = x.shape
    depth = w_ada.shape[0]
    sbw = sb_norm_w.shape[1]
    sb_heads = sbw // SB_HEAD_DIM
    ssd_heads = dt_bias.shape[1]
    n_main = w_in.shape[2] - ssd_heads
    alpha = (2 * depth) ** 0.25
    tb = 512 if (batch * seq * TOP_K) // w_gate.shape[1] >= 2048 else 128

    mod_all = _ada_call(c, w_ada, b_ada).reshape(depth, batch, 6, d)
    x2d = x.reshape(batch * seq, d)
    for l in range(depth):
        mod = mod_all[l]
        w_main = w_in[l, :, :n_main].astype(BF16)
        w_dt = jnp.pad(w_in[l, :, n_main:], ((0, 0), (0, 128 - ssd_heads))).astype(BF16)
        proj, dtp = _inproj_call(x2d, mod, w_main, w_dt, seq)
        o_sb = _sb_call(proj, sb_norm_w[l], batch, seq, sb_heads)
        o_ssd = _ssd_call(proj, dtp[:, :ssd_heads], conv_w[l], conv_b[l], dt_bias[l], a_log[l], d_skip[l],
                          ssd_norm_w[l], batch, seq, sbw)
        x1, h2 = _outproj_call(o_sb, o_ssd, x2d, mod, w_out[l].astype(BF16), ln1_g[l], ln1_b[l], seq, alpha)
        x2d = _moe(h2, x1, mod, w_router[l].T.astype(BF16), router_bias[l], w_gate, w_up, w_down, l,
                   ws_gate[l].astype(BF16), ws_up[l].astype(BF16), ws_down[l].astype(BF16), ln2_g[l], ln2_b[l],
                   seq, alpha, tb)
    return x2d.reshape(batch, seq, d)
```

```python
import functools
import math

import jax
import jax.numpy as jnp
from jax import lax
from jax.experimental import pallas as pl
from jax.experimental.pallas import tpu as pltpu

F32 = jnp.float32
BF16 = jnp.bfloat16

SB_HEAD_DIM = 128
SSD_HEAD_DIM = 64
SSD_GROUPS = 2
SSD_STATE = 128
SSD_CHUNK = 128
TOP_K = 8
N_EXPERT_GROUPS = 8
TOPK_GROUPS = 4
ROUTED_SCALE = 2.5
LN_EPS = 1e-5
RMS_EPS = 1e-6
SB_DEAD_LOG = -110.0
META_W = 128
META_TOK = 1
META_IDX = 8

VMEM_LIMIT_BYTES = 56 * 1024 * 1024


def _params(sem, vmem=VMEM_LIMIT_BYTES):
    return pltpu.CompilerParams(dimension_semantics=sem, vmem_limit_bytes=vmem)


def _pick(n, cands):
    for c in cands:
        if n % c == 0:
            return c
    raise ValueError(f"no tile in {cands} divides {n}")


def _softplus(z):
    return jnp.maximum(z, 0.0) + jnp.log(1.0 + jnp.exp(-jnp.abs(z)))


def _silu(v):
    return v * jax.nn.sigmoid(v)


def _split3(v):
    hi = v.astype(BF16)
    r1 = v - hi.astype(F32)
    mid = r1.astype(BF16)
    lo = (r1 - mid.astype(F32)).astype(BF16)
    return hi, mid, lo


def _dot(a, b):
    return jnp.dot(a, b, preferred_element_type=F32)


def _dot_nt(a, b):
    return lax.dot_general(a, b, (((1,), (1,)), ((), ())), preferred_element_type=F32)


def _dot_tn(a, b):
    return lax.dot_general(a, b, (((0,), (0,)), ((), ())), preferred_element_type=F32)


def _pack_halves(v):
    half = v.shape[1] // 2
    hi = lax.bitcast_convert_type(v[:, :half].astype(BF16).astype(F32), jnp.uint32)
    lo = lax.bitcast_convert_type(v[:, half:].astype(BF16).astype(F32), jnp.uint32)
    return hi | (lo >> 16)


def _unpack_halves(w):
    hi = lax.bitcast_convert_type(w & jnp.uint32(0xFFFF0000), F32)
    lo = lax.bitcast_convert_type(w << 16, F32)
    return hi, lo


def _ln_stats(v):
    mu = jnp.mean(v, axis=-1, keepdims=True)
    d = v - mu
    var = jnp.mean(d * d, axis=-1, keepdims=True)
    return d * lax.rsqrt(var + LN_EPS)


def _ada_kernel(c_ref, w_ref, b_ref, o_ref):
    ca = _silu(c_ref[...]).astype(BF16)
    o_ref[0] = _dot(ca, w_ref[0].astype(BF16)) + b_ref[0]


def _ada_call(c, w_ada, b_ada):
    depth, d, n = w_ada.shape
    b = c.shape[0]
    tn = _pick(n, (512, 256, 128))
    return pl.pallas_call(
        _ada_kernel,
        out_shape=jax.ShapeDtypeStruct((depth, b, n), F32),
        grid=(depth, n // tn),
        in_specs=[
            pl.BlockSpec((b, d), lambda l, j: (0, 0)),
            pl.BlockSpec((1, d, tn), lambda l, j: (l, 0, j)),
            pl.BlockSpec((1, 1, tn), lambda l, j: (l, 0, j)),
        ],
        out_specs=pl.BlockSpec((1, b, tn), lambda l, j: (l, 0, j)),
        compiler_params=_params(("arbitrary", "arbitrary")),
        name="ada_mod",
    )(c, w_ada, b_ada.reshape(depth, 1, n))


def _inproj_kernel(x_ref, mod_ref, w_ref, wdt_ref, o_ref, dt_ref, h_scr):
    @pl.when(pl.program_id(1) == 0)
    def _():
        xn = _ln_stats(x_ref[...])
        h = xn * (1.0 + mod_ref[0, 1:2, :]) + mod_ref[0, 0:1, :]
        hb = h.astype(BF16)
        h_scr[...] = hb
        dt_ref[...] = _dot(hb, wdt_ref[...])

    o_ref[...] = _dot(h_scr[...], w_ref[...]).astype(BF16)


def _inproj_call(x2d, mod, w_main, w_dt, seq):
    t, d = x2d.shape
    n = w_main.shape[1]
    tm = _pick(seq, (1024, 512, 256, 128))
    tn = _pick(n, (512, 256, 128))
    per_b = seq // tm
    return pl.pallas_call(
        _inproj_kernel,
        out_shape=(jax.ShapeDtypeStruct((t, n), BF16), jax.ShapeDtypeStruct((t, 128), F32)),
        grid=(t // tm, n // tn),
        in_specs=[
            pl.BlockSpec((tm, d), lambda i, j: (i, 0)),
            pl.BlockSpec((1, 6, d), lambda i, j: (i // per_b, 0, 0)),
            pl.BlockSpec((d, tn), lambda i, j: (0, j)),
            pl.BlockSpec((d, 128), lambda i, j: (0, 0)),
        ],
        out_specs=(
            pl.BlockSpec((tm, tn), lambda i, j: (i, j)),
            pl.BlockSpec((tm, 128), lambda i, j: (i, 0)),
        ),
        scratch_shapes=[pltpu.VMEM((tm, d), BF16)],
        compiler_params=_params(("arbitrary", "arbitrary")),
        name="in_proj",
    )(x2d, mod, w_main, w_dt)


def _sb_kernel(q_ref, k_ref, v_ref, nw_ref, o_ref, *, tq, tk, hg):
    qi = pl.program_id(2)
    hd = SB_HEAD_DIM
    scale = 1.0 / math.sqrt(hd)
    nd = tq // tk
    rj = lax.broadcasted_iota(jnp.int32, (tq, tk), 0)
    cs = lax.broadcasted_iota(jnp.int32, (tq, tk), 1)
    lr = lax.broadcasted_iota(jnp.int32, (tk, tk), 0)
    lc = lax.broadcasted_iota(jnp.int32, (tk, tk), 1)
    later = jnp.where(lr > lc, 1.0, 0.0).astype(BF16)
    qs = [q_ref[:, g * hd:(g + 1) * hd] for g in range(hg)]

    def block(kb, accs, runs, offset):
        start = pl.multiple_of(kb * tk, tk)
        new_accs, new_runs = [], []
        for g in range(hg):
            k = k_ref[pl.ds(start, tk), g * hd:(g + 1) * hd]
            v = v_ref[pl.ds(start, tk), g * hd:(g + 1) * hd]
            z = _dot_nt(qs[g], k) * scale
            sp = _softplus(z)
            if offset is None:
                lom = -sp
            else:
                mask = (cs + offset) < rj
                lom = jnp.where(mask, -sp, 0.0)
            hi = lom.astype(BF16)
            lo = (lom - hi.astype(F32)).astype(BF16)
            between = _dot(hi, later) + _dot(lo, later) + runs[g]
            w = jnp.exp(z - sp + between)
            if offset is not None:
                w = jnp.where(mask, w, 0.0)
            new_accs.append(accs[g] + _dot(w.astype(BF16), v))
            new_runs.append(runs[g] + jnp.sum(lom, axis=1, keepdims=True))
        return new_accs, new_runs

    accs = [jnp.zeros((tq, hd), F32) for _ in range(hg)]
    runs = [jnp.zeros((tq, 1), F32) for _ in range(hg)]
    for j in range(nd):
        accs, runs = block(qi * nd + (nd - 1 - j), accs, runs, (nd - 1 - j) * tk)

    def run_max(rs):
        m = rs[0]
        for r in rs[1:]:
            m = jnp.maximum(m, r)
        return jnp.max(m)

    def cond(carry):
        i, top, _, _ = carry
        return jnp.logical_and(i < qi * nd, top > SB_DEAD_LOG)

    def body(carry):
        i, _, accs, runs = carry
        accs, runs = block(qi * nd - 1 - i, list(accs), list(runs), None)
        return i + 1, run_max(runs), tuple(accs), tuple(runs)

    _, _, accs, _ = lax.while_loop(cond, body, (jnp.int32(0), run_max(runs), tuple(accs), tuple(runs)))
    for g in range(hg):
        acc = accs[g]
        o = acc * lax.rsqrt(jnp.mean(acc * acc, axis=-1, keepdims=True) + RMS_EPS) * nw_ref[:, g * hd:(g + 1) * hd]
        o_ref[:, g * hd:(g + 1) * hd] = o.astype(BF16)


def _sb_call(proj, sb_norm_w, batch, seq, heads):
    t = proj.shape[0]
    tq = _pick(seq, (256, 128))
    tk = tq
    hg = 4 if heads % 4 == 0 else 2
    nq = seq // tq
    hb = heads // hg
    wd = hg * SB_HEAD_DIM
    return pl.pallas_call(
        functools.partial(_sb_kernel, tq=tq, tk=tk, hg=hg),
        out_shape=jax.ShapeDtypeStruct((t, heads * SB_HEAD_DIM), BF16),
        grid=(batch, hb, nq),
        in_specs=[
            pl.BlockSpec((tq, wd), lambda b, h, i: (b * nq + i, h)),
            pl.BlockSpec((seq, wd), lambda b, h, i: (b, hb + h)),
            pl.BlockSpec((seq, wd), lambda b, h, i: (b, 2 * hb + h)),
            pl.BlockSpec((1, wd), lambda b, h, i: (0, h)),
        ],
        out_specs=pl.BlockSpec((tq, wd), lambda b, h, i: (b * nq + i, h)),
        compiler_params=_params(("arbitrary", "arbitrary", "arbitrary")),
        name="sb_attn",
    )(proj, proj, proj, sb_norm_w.reshape(1, -1))


def _ssd_kernel(z_ref, xs_ref, b_ref, c_ref, dt_ref, dtt_ref, cw_ref, cb_ref, dtb_ref, dtbt_ref,
                alog_ref, alogt_ref, dskip_ref, nw_ref, o_ref, prev_scr, state_scr, *, heads):
    ln = xs_ref.shape[0]
    w = xs_ref.shape[1]
    gn = b_ref.shape[1]
    n = gn // SSD_GROUPS
    wg = w // SSD_GROUPS
    hpg = heads // SSD_GROUPS

    @pl.when(pl.program_id(1) == 0)
    def _():
        prev_scr[...] = jnp.zeros_like(prev_scr)
        state_scr[...] = jnp.zeros_like(state_scr)

    raw = jnp.concatenate([xs_ref[...], b_ref[...], c_ref[...]], axis=1).astype(F32)
    pr = prev_scr[...]
    ch = raw.shape[1]
    row = lax.broadcasted_iota(jnp.int32, (ln, ch), 0)
    taps = cw_ref.shape[0]
    acc = raw * cw_ref[taps - 1:taps, :] + cb_ref[...]
    for sh in range(1, taps):
        shifted = jnp.where(row < sh, pltpu.roll(pr, sh, 0), pltpu.roll(raw, sh, 0))
        acc = acc + shifted * cw_ref[taps - 1 - sh:taps - sh, :]
    prev_scr[...] = raw
    xbc = _silu(acc)
    xs = xbc[:, :w]
    bm = xbc[:, w:w + gn]
    cm = xbc[:, w + gn:]

    dtv = _softplus(dt_ref[0] + dtb_ref[...])
    dtvt = _softplus(dtt_ref[0] + dtbt_ref[...])
    a_full = -jnp.exp(alog_ref[...])
    a_t = -jnp.exp(alogt_ref[...])

    eh = lax.broadcasted_iota(jnp.int32, (heads, w), 0)
    ec = lax.broadcasted_iota(jnp.int32, (heads, w), 1)
    expand = jnp.where(ec // SSD_HEAD_DIM == eh, 1.0, 0.0).astype(BF16)
    d_hi, d_mid, d_lo = _split3(dtv)
    dt_exp = _dot(d_hi, expand) + _dot(d_mid, expand) + _dot(d_lo, expand)

    rl = lax.broadcasted_iota(jnp.int32, (ln, ln), 0)
    cl = lax.broadcasted_iota(jnp.int32, (ln, ln), 1)
    tril = cl <= rl
    incl = jnp.where(tril, 1.0, 0.0).astype(BF16)
    incl_t = jnp.where(rl <= cl, 1.0, 0.0).astype(BF16)

    a_exp = dt_exp * a_full
    a_hi, a_mid, a_lo = _split3(a_exp)
    acum = _dot(incl, a_hi) + _dot(incl, a_mid) + _dot(incl, a_lo)
    at = dtvt * a_t
    t_hi, t_mid, t_lo = _split3(at)
    acum_t = _dot(t_hi, incl_t) + _dot(t_mid, incl_t) + _dot(t_lo, incl_t)

    xdt = xs * dt_exp
    last = acum[ln - 1:ln, :]
    xdd = (xdt * jnp.exp(last - acum)).astype(BF16)
    xdtb = xdt.astype(BF16)
    ea = jnp.exp(acum)
    cdec = jnp.exp(last)

    col_head = lax.broadcasted_iota(jnp.int32, (ln, wg), 1) // SSD_HEAD_DIM
    ys = []
    for g in range(SSD_GROUPS):
        bg = bm[:, g * n:(g + 1) * n].astype(BF16)
        cg = cm[:, g * n:(g + 1) * n].astype(BF16)
        cb = _dot_nt(cg, bg)
        ms = []
        for e in range(hpg):
            h = g * hpg + e
            ac = acum[:, h * SSD_HEAD_DIM:h * SSD_HEAD_DIM + 1]
            ar = acum_t[h:h + 1, :]
            lmat = jnp.exp(jnp.where(tril, ac - ar, -jnp.inf))
            ms.append((cb * lmat).astype(BF16))
        mcat = jnp.concatenate(ms, axis=1)
        xg = xdtb[:, g * wg:(g + 1) * wg]
        xbd = jnp.concatenate([jnp.where(col_head == e, xg, jnp.zeros_like(xg)) for e in range(hpg)], axis=0)
        y_diag = _dot(mcat, xbd)
        sg = state_scr[g]
        y_off = _dot(cg, sg.astype(BF16)) * ea[:, g * wg:(g + 1) * wg]
        s_new = _dot_tn(bg, xdd[:, g * wg:(g + 1) * wg])
        state_scr[g] = sg * cdec[:, g * wg:(g + 1) * wg] + s_new
        ys.append(y_diag + y_off)
    y = jnp.concatenate(ys, axis=1) + xs * dskip_ref[...]
    y = y * _silu(z_ref[...].astype(F32))
    o = y * lax.rsqrt(jnp.mean(y * y, axis=-1, keepdims=True) + RMS_EPS) * nw_ref[...]
    o_ref[...] = o.astype(BF16)


def _ssd_call(proj, dt, conv_w, conv_b, dt_bias, a_log, d_skip, ssd_norm_w, batch, seq, sbw):
    t = proj.shape[0]
    heads = dt_bias.shape[0]
    w = heads * SSD_HEAD_DIM
    gn = SSD_GROUPS * SSD_STATE
    ch = w + 2 * gn
    ln = SSD_CHUNK
    nc = seq // ln
    dt3 = dt.reshape(batch, seq, heads)
    dtt = jnp.swapaxes(dt3, 1, 2)
    rep = lambda v: jnp.repeat(v, SSD_HEAD_DIM).reshape(1, w)
    z_blk, xs_blk = 3 * sbw // w, (3 * sbw + w) // w
    b_blk, c_blk = (3 * sbw + 2 * w) // gn, (3 * sbw + 2 * w + gn) // gn
    assert 3 * sbw % w == 0 and (3 * sbw + 2 * w) % gn == 0
    full = lambda shape: pl.BlockSpec(shape, lambda b, c: (0,) * len(shape))
    return pl.pallas_call(
        functools.partial(_ssd_kernel, heads=heads),
        out_shape=jax.ShapeDtypeStruct((t, w), BF16),
        grid=(batch, nc),
        in_specs=[
            pl.BlockSpec((ln, w), lambda b, c: (b * nc + c, z_blk)),
            pl.BlockSpec((ln, w), lambda b, c: (b * nc + c, xs_blk)),
            pl.BlockSpec((ln, gn), lambda b, c: (b * nc + c, b_blk)),
            pl.BlockSpec((ln, gn), lambda b, c: (b * nc + c, c_blk)),
            pl.BlockSpec((1, ln, heads), lambda b, c: (b, c, 0)),
            pl.BlockSpec((1, heads, ln), lambda b, c: (b, 0, c)),
            full(conv_w.shape), full((1, ch)), full((1, heads)), full((heads, 1)),
            full((1, w)), full((heads, 1)), full((1, w)), full((1, w)),
        ],
        out_specs=pl.BlockSpec((ln, w), lambda b, c: (b * nc + c, 0)),
        scratch_shapes=[pltpu.VMEM((ln, ch), F32), pltpu.VMEM((SSD_GROUPS, SSD_STATE, w // SSD_GROUPS), F32)],
        compiler_params=_params(("arbitrary", "arbitrary")),
        name="ssd",
    )(proj, proj, proj, proj, dt3, dtt, conv_w, conv_b.reshape(1, ch), dt_bias.reshape(1, heads),
      dt_bias.reshape(heads, 1), rep(a_log), a_log.reshape(heads, 1), rep(d_skip), ssd_norm_w.reshape(1, w))


def _outproj_kernel(osb_ref, ossd_ref, x_ref, mod_ref, w1_ref, w2_ref, g_ref, b_ref, x1_ref, h2_ref, *, alpha):
    m = _dot(osb_ref[...], w1_ref[...]) + _dot(ossd_ref[...], w2_ref[...])
    v = alpha * x_ref[...] + (1.0 + mod_ref[0, 2:3, :]) * m
    x1 = _ln_stats(v) * g_ref[...] + b_ref[...]
    x1_ref[...] = x1
    h2_ref[...] = _pack_halves(_ln_stats(x1) * (1.0 + mod_ref[0, 4:5, :]) + mod_ref[0, 3:4, :])


def _outproj_call(o_sb, o_ssd, x2d, mod, w_out_b, ln_g, ln_b, seq, alpha):
    t, d = x2d.shape
    sbw, ssw = o_sb.shape[1], o_ssd.shape[1]
    tm = _pick(seq, (256, 128))
    per_b = seq // tm
    return pl.pallas_call(
        functools.partial(_outproj_kernel, alpha=alpha),
        out_shape=(jax.ShapeDtypeStruct((t, d), F32), jax.ShapeDtypeStruct((t, d // 2), jnp.uint32)),
        grid=(t // tm,),
        in_specs=[
            pl.BlockSpec((tm, sbw), lambda i: (i, 0)),
            pl.BlockSpec((tm, ssw), lambda i: (i, 0)),
            pl.BlockSpec((tm, d), lambda i: (i, 0)),
            pl.BlockSpec((1, 6, d), lambda i: (i // per_b, 0, 0)),
            pl.BlockSpec((sbw, d), lambda i: (0, 0)),
            pl.BlockSpec((ssw, d), lambda i: (0, 0)),
            pl.BlockSpec((1, d), lambda i: (0, 0)),
            pl.BlockSpec((1, d), lambda i: (0, 0)),
        ],
        out_specs=(pl.BlockSpec((tm, d), lambda i: (i, 0)), pl.BlockSpec((tm, d // 2), lambda i: (i, 0))),
        compiler_params=_params(("arbitrary",)),
        name="out_proj",
    )(o_sb, o_ssd, x2d, mod, w_out_b[:sbw], w_out_b[sbw:], ln_g.reshape(1, d), ln_b.reshape(1, d))


def _router_kernel(h_ref, wr_ref, bias_ref, idx_ref, rnk_ref, gate_ref, cnt_ref, carry_scr):
    ne = wr_ref.shape[0]
    tm = h_ref.shape[0]
    per_g = ne // N_EXPERT_GROUPS
    neg = -jnp.inf

    @pl.when(pl.program_id(0) == 0)
    def _():
        carry_scr[...] = jnp.zeros_like(carry_scr)

    ha, hb = _unpack_halves(h_ref[...])
    half = ha.shape[1]
    logits = _dot_nt(wr_ref[:, :half], ha.astype(BF16)) + _dot_nt(wr_ref[:, half:], hb.astype(BF16))
    scores = jax.nn.sigmoid(logits)
    biased = scores + bias_ref[...]
    s3 = scores.reshape(N_EXPERT_GROUPS, per_g, tm)
    b3 = biased.reshape(N_EXPERT_GROUPS, per_g, tm)
    shape3 = (N_EXPERT_GROUPS, per_g, tm)
    j_io = lax.broadcasted_iota(jnp.int32, shape3, 1)
    g_io = lax.broadcasted_iota(jnp.int32, shape3, 0)
    e_io = g_io * per_g + j_io

    m1 = jnp.max(b3, axis=1, keepdims=True)
    i1 = jnp.min(jnp.where(b3 == m1, j_io, per_g), axis=1, keepdims=True)
    m2 = jnp.max(jnp.where(j_io == i1, neg, b3), axis=1, keepdims=True)
    gs = m1 + m2
    gg = lax.broadcasted_iota(jnp.int32, gs.shape, 0)
    gmask = jnp.zeros(gs.shape, F32)
    cur = gs
    for _ in range(TOPK_GROUPS):
        m = jnp.max(cur, axis=0, keepdims=True)
        gi = jnp.min(jnp.where(cur == m, gg, N_EXPERT_GROUPS), axis=0, keepdims=True)
        pick = gg == gi
        gmask = jnp.where(pick, 1.0, gmask)
        cur = jnp.where(pick, neg, cur)
    cur = jnp.where(jnp.broadcast_to(gmask, shape3) > 0.5, b3, neg)

    sel = jnp.zeros(shape3, F32)
    picks = []
    for _ in range(TOP_K):
        m = jnp.max(jnp.max(cur, axis=1, keepdims=True), axis=0, keepdims=True)
        ei = jnp.min(jnp.min(jnp.where(cur == m, e_io, ne), axis=1, keepdims=True), axis=0, keepdims=True)
        pick = e_io == ei
        sel = jnp.where(pick, 1.0, sel)
        cur = jnp.where(pick, neg, cur)
        picks.append((pick, ei))

    selw = sel * s3
    denom = jnp.sum(jnp.sum(selw, axis=1, keepdims=True), axis=0, keepdims=True)
    gates3 = selw / denom * ROUTED_SCALE

    sel2 = sel.reshape(ne, tm)
    rj = lax.broadcasted_iota(jnp.int32, (tm, tm), 0)
    cs = lax.broadcasted_iota(jnp.int32, (tm, tm), 1)
    before = jnp.where(rj < cs, 1.0, 0.0).astype(BF16)
    carry = carry_scr[...]
    rank2 = _dot(sel2.astype(BF16), before) + carry[:, 0:1]
    rank3 = rank2.reshape(shape3)
    new_carry = carry + jnp.sum(sel2, axis=1, keepdims=True)
    carry_scr[...] = new_carry
    cnt_ref[...] = new_carry.astype(jnp.int32)

    for k, (pick, ei) in enumerate(picks):
        red = lambda v: jnp.sum(jnp.sum(v, axis=1, keepdims=True), axis=0, keepdims=True).reshape(1, tm)
        idx_ref[k:k + 1, :] = ei.reshape(1, tm)
        rnk_ref[k:k + 1, :] = red(jnp.where(pick, rank3, 0.0)).astype(jnp.int32)
        gate_ref[k:k + 1, :] = red(jnp.where(pick, gates3, 0.0))


def _router_call(h2p, wr_t, bias):
    t = h2p.shape[0]
    ne, d = wr_t.shape
    tm = _pick(t, (512, 256))
    if t == tm:
        tm = tm // 2
    return pl.pallas_call(
        _router_kernel,
        out_shape=(jax.ShapeDtypeStruct((TOP_K, t), jnp.int32), jax.ShapeDtypeStruct((TOP_K, t), jnp.int32),
                   jax.ShapeDtypeStruct((TOP_K, t), F32), jax.ShapeDtypeStruct((ne, 128), jnp.int32)),
        grid=(t // tm,),
        in_specs=[
            pl.BlockSpec((tm, d // 2), lambda i: (i, 0)),
            pl.BlockSpec((ne, d), lambda i: (0, 0)),
            pl.BlockSpec((ne, 1), lambda i: (0, 0)),
        ],
        out_specs=(pl.BlockSpec((TOP_K, tm), lambda i: (0, i)), pl.BlockSpec((TOP_K, tm), lambda i: (0, i)),
                   pl.BlockSpec((TOP_K, tm), lambda i: (0, i)), pl.BlockSpec((ne, 128), lambda i: (0, 0))),
        scratch_shapes=[pltpu.VMEM((ne, 128), F32)],
        compiler_params=_params(("arbitrary",)),
        name="router",
    )(h2p, wr_t, bias.reshape(ne, 1))


def _dispatch_kernel(ps_ref, pc_ref, nu_ref, dest_ref, h_ref, meta_ref, xs_hbm, zbuf, rowbuf, idx_smem,
                     isem, ssem, zsem, tsem):
    tm = h_ref.shape[0]
    pw = h_ref.shape[1]
    ne = ps_ref.shape[0]
    tb = zbuf.shape[0]
    nblk = xs_hbm.shape[0] // tb

    @pl.when(pl.program_id(0) == 0)
    def _():
        zbuf[...] = jnp.zeros_like(zbuf)

        def tail(r, c):
            pltpu.make_async_copy(zbuf, xs_hbm.at[pl.ds(pl.multiple_of(r * tb, tb), tb)], tsem).start()
            return c

        def tail_drain(r, c):
            pltpu.make_async_copy(zbuf, xs_hbm.at[pl.ds(0, tb)], tsem).wait()
            return c

        lax.fori_loop(nu_ref[0], nblk, tail, 0)
        lax.fori_loop(nu_ref[0], nblk, tail_drain, 0)

        def fill(e, c):
            def one(j, c2):
                pltpu.make_async_copy(zbuf.at[0], xs_hbm.at[ps_ref[e] + j], zsem).start()
                return c2
            return lax.fori_loop(0, pc_ref[e], one, c)

        def drain(e, c):
            def one(j, c2):
                pltpu.make_async_copy(zbuf.at[0], xs_hbm.at[0], zsem).wait()
                return c2
            return lax.fori_loop(0, pc_ref[e], one, c)

        lax.fori_loop(0, ne, fill, 0)
        lax.fori_loop(0, ne, drain, 0)

    cp = pltpu.make_async_copy(dest_ref.at[0, 0], idx_smem, isem)
    cp.start()
    cp.wait()
    rowbuf[:, :pw] = h_ref[...]
    rowbuf[:, pw:] = meta_ref[...]

    def issue(i, c):
        for k in range(TOP_K):
            pltpu.make_async_copy(rowbuf.at[i], xs_hbm.at[idx_smem[k * tm + i]], ssem).start()
        return c

    lax.fori_loop(0, tm, issue, 0, unroll=2)
    pltpu.make_async_copy(xs_hbm.at[pl.ds(0, TOP_K * tm)], xs_hbm.at[pl.ds(0, TOP_K * tm)], ssem).wait()


def _dispatch_call(pad_start, pad_cnt, n_used, dest, h2p, meta, n_rows, tb):
    t, pw = h2p.shape
    d = pw + meta.shape[1]
    tm = _pick(t, (512, 256))
    nt = t // tm
    dest_t = dest.reshape(TOP_K, nt, tm).transpose(1, 0, 2).reshape(nt, 1, TOP_K * tm)
    grid_spec = pltpu.PrefetchScalarGridSpec(
        num_scalar_prefetch=3,
        grid=(nt,),
        in_specs=[
            pl.BlockSpec((1, 1, TOP_K * tm), lambda i, ps, pc, nu: (i, 0, 0)),
            pl.BlockSpec((tm, pw), lambda i, ps, pc, nu: (i, 0)),
            pl.BlockSpec((tm, meta.shape[1]), lambda i, ps, pc, nu: (i, 0)),
        ],
        out_specs=pl.BlockSpec(memory_space=pl.ANY),
        scratch_shapes=[pltpu.VMEM((tb, d), h2p.dtype), pltpu.VMEM((tm, d), h2p.dtype),
                        pltpu.SMEM((TOP_K * tm,), jnp.int32), pltpu.SemaphoreType.DMA,
                        pltpu.SemaphoreType.DMA, pltpu.SemaphoreType.DMA, pltpu.SemaphoreType.DMA],
    )
    return pl.pallas_call(
        _dispatch_kernel,
        out_shape=jax.ShapeDtypeStruct((n_rows, d), h2p.dtype),
        grid_spec=grid_spec,
        compiler_params=_params(("arbitrary",)),
        name="dispatch",
    )(pad_start, pad_cnt, n_used, dest_t, h2p, meta)


def _expert_kernel(be_ref, nu_ref, x_ref, wg_ref, wu_ref, wd_ref, y_hbm, wgb, wub, wdb, obuf, dstv, dst_smem,
                   isem, osem, *, tm, n_slots):
    r = pl.program_id(0)
    tb = x_ref.shape[0]
    pw = x_ref.shape[1] - META_W
    slot = r % 2
    prev = 1 - slot
    last = nu_ref[0] - 1

    def wait_rows(s):
        pltpu.make_async_copy(obuf.at[s], y_hbm.at[pl.ds(0, tb)], osem.at[s]).wait()

    def start_row(s, i):
        pltpu.make_async_copy(obuf.at[s, i], y_hbm.at[dst_smem[s, i]], osem.at[s]).start()

    def index_copy(s):
        return pltpu.make_async_copy(dstv.at[0], dst_smem.at[s], isem)

    @pl.when(r == 0)
    def _():
        obuf[...] = jnp.zeros_like(obuf)
        cp = pltpu.make_async_copy(obuf.at[0], y_hbm.at[pl.ds(n_slots, tb)], osem.at[0])
        cp.start()
        cp.wait()

        def init(i, c):
            dst_smem[1, i] = n_slots + tb + i
            return c

        lax.fori_loop(0, tb, init, 0)

    @pl.when(r <= last)
    def _():
        @pl.when(r >= 1)
        def _():
            wait_rows(slot)
            index_copy(prev).wait()

        @pl.when(jnp.logical_or(r == 0, be_ref[r] != be_ref[jnp.maximum(r - 1, 0)]))
        def _():
            wgb[...] = wg_ref[0, 0].astype(BF16)
            wub[...] = wu_ref[0, 0].astype(BF16)
            wdb[...] = wd_ref[0, 0].astype(BF16)

        for i in range(tb):
            start_row(prev, i)

        xa, xb = _unpack_halves(x_ref[:, :pw])
        xa, xb = xa.astype(BF16), xb.astype(BF16)
        g = _dot(xa, wgb[:pw, :]) + _dot(xb, wgb[pw:, :])
        u = _dot(xa, wub[:pw, :]) + _dot(xb, wub[pw:, :])
        obuf[slot] = _pack_halves(_dot((_silu(g) * u).astype(BF16), wdb[...]))

        meta = lax.bitcast_convert_type(x_ref[:, pw:], jnp.int32)
        lane = lax.broadcasted_iota(jnp.int32, meta.shape, 1)
        hit = jnp.logical_and(meta == be_ref[r], jnp.logical_and(lane >= META_IDX, lane < META_IDX + TOP_K))
        kpos = jnp.sum(jnp.where(hit, lane - META_IDX, 0), axis=1, keepdims=True)
        tok = meta[:, META_TOK:META_TOK + 1]
        spare = n_slots + slot * tb + lax.broadcasted_iota(jnp.int32, (tb, 1), 0)
        home = (jnp.right_shift(tok, tm.bit_length() - 1) * TOP_K + kpos) * tm + jnp.bitwise_and(tok, tm - 1)
        dst = jnp.where(meta[:, 0:1] == 1, home, spare)
        dst_t = jnp.transpose(jnp.broadcast_to(dst.astype(F32), (tb, META_W)))
        dstv[...] = dst_t[0:8, :].astype(jnp.int32)
        index_copy(slot).start()

        @pl.when(r == last)
        def _():
            index_copy(slot).wait()

            def issue(i, c):
                start_row(slot, i)
                return c

            lax.fori_loop(0, tb, issue, 0, unroll=8)
            wait_rows(prev)
            wait_rows(slot)


def _expert_call(block_expert, n_used, xs, w_gate, w_up, w_down, layer, tb, tm, n_tok):
    n_rows, xw = xs.shape
    pw = xw - META_W
    d, de = w_gate.shape[2:]
    nblk = n_rows // tb
    n_slots = n_tok * TOP_K
    assert tm & (tm - 1) == 0
    blk = lambda r, be, nu: jnp.minimum(r, nu[0] - 1)
    wmap = lambda r, be, nu: (layer, be[blk(r, be, nu)], 0, 0)
    grid_spec = pltpu.PrefetchScalarGridSpec(
        num_scalar_prefetch=2,
        grid=(nblk,),
        in_specs=[
            pl.BlockSpec((tb, xw), lambda r, be, nu: (blk(r, be, nu), 0)),
            pl.BlockSpec((1, 1, d, de), wmap),
            pl.BlockSpec((1, 1, d, de), wmap),
            pl.BlockSpec((1, 1, de, d), wmap),
        ],
        out_specs=pl.BlockSpec(memory_space=pl.ANY),
        scratch_shapes=[pltpu.VMEM((d, de), BF16), pltpu.VMEM((d, de), BF16), pltpu.VMEM((de, d), BF16),
                        pltpu.VMEM((2, tb, pw), jnp.uint32), pltpu.VMEM((8, tb), jnp.int32),
                        pltpu.SMEM((2, tb), jnp.int32), pltpu.SemaphoreType.DMA, pltpu.SemaphoreType.DMA((2,))],
    )
    return pl.pallas_call(
        functools.partial(_expert_kernel, tm=tm, n_slots=n_slots),
        out_shape=jax.ShapeDtypeStruct((n_slots + 2 * tb, pw), jnp.uint32),
        grid_spec=grid_spec,
        compiler_params=_params(("arbitrary",)),
        name="experts",
    )(block_expert, n_used, xs, w_gate, w_up, w_down)


def _combine_kernel(y_ref, gate_ref, h_ref, x_ref, mod_ref, sg_ref, su_ref, sd_ref, g_ref, b_ref, out_ref, *, alpha):
    tm = h_ref.shape[0]
    ha, hb = _unpack_halves(h_ref[...])
    ha, hb = ha.astype(BF16), hb.astype(BF16)
    half = ha.shape[1]
    sg = _dot(ha, sg_ref[:half, :]) + _dot(hb, sg_ref[half:, :])
    su = _dot(ha, su_ref[:half, :]) + _dot(hb, su_ref[half:, :])
    f = _dot((_silu(sg) * su).astype(BF16), sd_ref[...])
    fa, fb = f[:, :half], f[:, half:]
    gate = gate_ref[...]
    for k in range(TOP_K):
        oa, ob = _unpack_halves(y_ref[pl.ds(k * tm, tm), :])
        fa = fa + oa * gate[:, k:k + 1]
        fb = fb + ob * gate[:, k:k + 1]
    f = jnp.concatenate([fa, fb], axis=1)
    v = alpha * x_ref[...] + (1.0 + mod_ref[0, 5:6, :]) * f
    out_ref[...] = _ln_stats(v) * g_ref[...] + b_ref[...]


def _combine_call(y8, gate_t, h2p, x1, mod, sg, su, sd, ln_g, ln_b, seq, alpha, tm):
    t, d = x1.shape
    ds = sg.shape[1]
    per_b = seq // tm
    return pl.pallas_call(
        functools.partial(_combine_kernel, alpha=alpha),
        out_shape=jax.ShapeDtypeStruct((t, d), F32),
        grid=(t // tm,),
        in_specs=[
            pl.BlockSpec((TOP_K * tm, d // 2), lambda i: (i, 0)),
            pl.BlockSpec((tm, TOP_K), lambda i: (i, 0)),
            pl.BlockSpec((tm, d // 2), lambda i: (i, 0)),
            pl.BlockSpec((tm, d), lambda i: (i, 0)),
            pl.BlockSpec((1, 6, d), lambda i: (i // per_b, 0, 0)),
            pl.BlockSpec((d, ds), lambda i: (0, 0)),
            pl.BlockSpec((d, ds), lambda i: (0, 0)),
            pl.BlockSpec((ds, d), lambda i: (0, 0)),
            pl.BlockSpec((1, d), lambda i: (0, 0)),
            pl.BlockSpec((1, d), lambda i: (0, 0)),
        ],
        out_specs=pl.BlockSpec((tm, d), lambda i: (i, 0)),
        compiler_params=_params(("arbitrary",)),
        name="combine",
    )(y8, gate_t, h2p, x1, mod, sg, su, sd, ln_g.reshape(1, d), ln_b.reshape(1, d))


def _moe(h2, x1, mod, wr_t, router_bias, w_gate, w_up, w_down, layer, sg, su, sd, ln_g, ln_b, seq, alpha, tb):
    t, d = x1.shape
    ne = w_gate.shape[1]
    idx, rnk, gate, cnt = _router_call(h2, wr_t, router_bias)
    counts = cnt[:, 0]
    padded = (counts + tb - 1) // tb * tb
    pend = jnp.cumsum(padded)
    pstart = pend - padded
    eids = jnp.arange(ne, dtype=jnp.int32)
    dest = rnk + jnp.sum(jnp.where(idx[..., None] == eids, pstart.astype(jnp.int32), 0), axis=-1)
    nblk = (t * TOP_K) // tb + ne
    blk_row = jnp.arange(nblk, dtype=jnp.int32) * tb
    block_expert = jnp.minimum(jnp.sum((pend[None, :] <= blk_row[:, None]).astype(jnp.int32), axis=1), ne - 1)
    n_used = (pend[-1] // tb).astype(jnp.int32).reshape(1)
    zeros = lambda n: jnp.zeros((t, n), jnp.int32)
    meta = jnp.concatenate([jnp.ones((t, 1), jnp.int32), jnp.arange(t, dtype=jnp.int32)[:, None],
                            zeros(META_IDX - 2), idx.T, zeros(META_W - META_IDX - TOP_K)], axis=1)
    xs = _dispatch_call((pstart + counts).astype(jnp.int32), (padded - counts).astype(jnp.int32), n_used, dest, h2,
                        lax.bitcast_convert_type(meta, jnp.uint32), nblk * tb, tb)
    tm = _pick(seq, (256, 128))
    y8 = _expert_call(block_expert, n_used, xs, w_gate, w_up, w_down, layer, tb, tm, t)
    return _combine_call(y8, gate.T, h2, x1, mod, sg, su, sd, ln_g, ln_b, seq, alpha, tm)


def kernel(x, c, w_ada, b_ada, w_in, conv_w, conv_b, dt_bias, a_log, d_skip, sb_norm_w, ssd_norm_w, w_out, ln1_g,
           ln1_b, w_router, router_bias, w_gate, w_up, w_down, ws_gate, ws_up, ws_down, ln2_g, ln2_b):
    batch, seq, d = x.shape
    depth = w_ada.shape[0]
    sbw = sb_norm_w.shape[1]
    sb_heads = sbw // SB_HEAD_DIM
    ssd_heads = dt_bias.shape[1]
    n_main = w_in.shape[2] - ssd_heads
    alpha = (2 * depth) ** 0.25
    tb = 512 if (batch * seq * TOP_K) // w_gate.shape[1] >= 2048 else 128

    mod_all = _ada_call(c, w_ada, b_ada).reshape(depth, batch, 6, d)
    x2d = x.reshape(batch * seq, d)
    for l in range(depth):
        mod = mod_all[l]
        w_main = w_in[l, :, :n_main].astype(BF16)
        w_dt = jnp.pad(w_in[l, :, n_main:], ((0, 0), (0, 128 - ssd_heads))).astype(BF16)
        proj, dtp = _inproj_call(x2d, mod, w_main, w_dt, seq)
        o_sb = _sb_call(proj, sb_norm_w[l], batch, seq, sb_heads)
        o_ssd = _ssd_call(proj, dtp[:, :ssd_heads], conv_w[l], conv_b[l], dt_bias[l], a_log[l], d_skip[l],
                          ssd_norm_w[l], batch, seq, sbw)
        x1, h2 = _outproj_call(o_sb, o_ssd, x2d, mod, w_out[l].astype(BF16), ln1_g[l], ln1_b[l], seq, alpha)
        x2d = _moe(h2, x1, mod, w_router[l].T.astype(BF16), router_bias[l], w_gate, w_up, w_down, l,
                   ws_gate[l].astype(BF16), ws_up[l].astype(BF16), ws_down[l].astype(BF16), ln2_g[l], ln2_b[l],
                   seq, alpha, tb)
    return x2d.reshape(batch, seq, d)
```

```python
import functools
import math

import jax
import jax.numpy as jnp
from jax import lax
from jax.experimental import pallas as pl
from jax.experimental.pallas import tpu as pltpu

F32 = jnp.float32
BF16 = jnp.bfloat16

SB_HEAD_DIM = 128
SSD_HEAD_DIM = 64
SSD_GROUPS = 2
SSD_STATE = 128
SSD_CHUNK = 128
TOP_K = 8
N_EXPERT_GROUPS = 8
TOPK_GROUPS = 4
ROUTED_SCALE = 2.5
LN_EPS = 1e-5
RMS_EPS = 1e-6
SB_DEAD_LOG = -110.0
META_W = 128
META_TOK = 1
META_IDX = 8

VMEM_LIMIT_BYTES = 56 * 1024 * 1024


def _params(sem, vmem=VMEM_LIMIT_BYTES):
    return pltpu.CompilerParams(dimension_semantics=sem, vmem_limit_bytes=vmem)


def _pick(n, cands):
    for c in cands:
        if n % c == 0:
            return c
    raise ValueError(f"no tile in {cands} divides {n}")


def _softplus(z):
    return jnp.maximum(z, 0.0) + jnp.log(1.0 + jnp.exp(-jnp.abs(z)))


def _silu(v):
    return v * jax.nn.sigmoid(v)


def _split3(v):
    hi = v.astype(BF16)
    r1 = v - hi.astype(F32)
    mid = r1.astype(BF16)
    lo = (r1 - mid.astype(F32)).astype(BF16)
    return hi, mid, lo


def _dot(a, b):
    return jnp.dot(a, b, preferred_element_type=F32)


def _dot_nt(a, b):
    return lax.dot_general(a, b, (((1,), (1,)), ((), ())), preferred_element_type=F32)


def _dot_tn(a, b):
    return lax.dot_general(a, b, (((0,), (0,)), ((), ())), preferred_element_type=F32)


def _pack_halves(v):
    half = v.shape[1] // 2
    hi = lax.bitcast_convert_type(v[:, :half].astype(BF16).astype(F32), jnp.uint32)
    lo = lax.bitcast_convert_type(v[:, half:].astype(BF16).astype(F32), jnp.uint32)
    return hi | (lo >> 16)


def _unpack_halves(w):
    hi = lax.bitcast_convert_type(w & jnp.uint32(0xFFFF0000), F32)
    lo = lax.bitcast_convert_type(w << 16, F32)
    return hi, lo


def _ln_stats(v):
    mu = jnp.mean(v, axis=-1, keepdims=True)
    d = v - mu
    var = jnp.mean(d * d, axis=-1, keepdims=True)
    return d * lax.rsqrt(var + LN_EPS)


def _ada_kernel(c_ref, w_ref, b_ref, o_ref):
    ca = _silu(c_ref[...]).astype(BF16)
    o_ref[0] = _dot(ca, w_ref[0].astype(BF16)) + b_ref[0]


def _ada_call(c, w_ada, b_ada):
    depth, d, n = w_ada.shape
    b = c.shape[0]
    tn = _pick(n, (512, 256, 128))
    return pl.pallas_call(
        _ada_kernel,
        out_shape=jax.ShapeDtypeStruct((depth, b, n), F32),
        grid=(depth, n // tn),
        in_specs=[
            pl.BlockSpec((b, d), lambda l, j: (0, 0)),
            pl.BlockSpec((1, d, tn), lambda l, j: (l, 0, j)),
            pl.BlockSpec((1, 1, tn), lambda l, j: (l, 0, j)),
        ],
        out_specs=pl.BlockSpec((1, b, tn), lambda l, j: (l, 0, j)),
        compiler_params=_params(("arbitrary", "arbitrary")),
        name="ada_mod",
    )(c, w_ada, b_ada.reshape(depth, 1, n))


def _inproj_kernel(x_ref, mod_ref, w_ref, wdt_ref, o_ref, dt_ref, h_scr):
    @pl.when(pl.program_id(1) == 0)
    def _():
        xn = _ln_stats(x_ref[...])
        h = xn * (1.0 + mod_ref[0, 1:2, :]) + mod_ref[0, 0:1, :]
        hb = h.astype(BF16)
        h_scr[...] = hb
        dt_ref[...] = _dot(hb, wdt_ref[...])

    o_ref[...] = _dot(h_scr[...], w_ref[...]).astype(BF16)


def _inproj_call(x2d, mod, w_main, w_dt, seq):
    t, d = x2d.shape
    n = w_main.shape[1]
    tm = _pick(seq, (1024, 512, 256, 128))
    tn = _pick(n, (512, 256, 128))
    per_b = seq // tm
    return pl.pallas_call(
        _inproj_kernel,
        out_shape=(jax.ShapeDtypeStruct((t, n), BF16), jax.ShapeDtypeStruct((t, 128), F32)),
        grid=(t // tm, n // tn),
        in_specs=[
            pl.BlockSpec((tm, d), lambda i, j: (i, 0)),
            pl.BlockSpec((1, 6, d), lambda i, j: (i // per_b, 0, 0)),
            pl.BlockSpec((d, tn), lambda i, j: (0, j)),
            pl.BlockSpec((d, 128), lambda i, j: (0, 0)),
        ],
        out_specs=(
            pl.BlockSpec((tm, tn), lambda i, j: (i, j)),
            pl.BlockSpec((tm, 128), lambda i, j: (i, 0)),
        ),
        scratch_shapes=[pltpu.VMEM((tm, d), BF16)],
        compiler_params=_params(("arbitrary", "arbitrary")),
        name="in_proj",
    )(x2d, mod, w_main, w_dt)


def _sb_kernel(q_ref, k_ref, v_ref, nw_ref, o_ref, *, tq, tk, hg):
    qi = pl.program_id(2)
    hd = SB_HEAD_DIM
    scale = 1.0 / math.sqrt(hd)
    nd = tq // tk
    rj = lax.broadcasted_iota(jnp.int32, (tq, tk), 0)
    cs = lax.broadcasted_iota(jnp.int32, (tq, tk), 1)
    lr = lax.broadcasted_iota(jnp.int32, (tk, tk), 0)
    lc = lax.broadcasted_iota(jnp.int32, (tk, tk), 1)
    later = jnp.where(lr > lc, 1.0, 0.0).astype(BF16)
    qs = [q_ref[:, g * hd:(g + 1) * hd] for g in range(hg)]

    def block(kb, accs, runs, offset):
        start = pl.multiple_of(kb * tk, tk)
        new_accs, new_runs = [], []
        for g in range(hg):
            k = k_ref[pl.ds(start, tk), g * hd:(g + 1) * hd]
            v = v_ref[pl.ds(start, tk), g * hd:(g + 1) * hd]
            z = _dot_nt(qs[g], k) * scale
            sp = _softplus(z)
            if offset is None:
                lom = -sp
            else:
                mask = (cs + offset) < rj
                lom = jnp.where(mask, -sp, 0.0)
            hi = lom.astype(BF16)
            lo = (lom - hi.astype(F32)).astype(BF16)
            between = _dot(hi, later) + _dot(lo, later) + runs[g]
            w = jnp.exp(z - sp + between)
            if offset is not None:
                w = jnp.where(mask, w, 0.0)
            new_accs.append(accs[g] + _dot(w.astype(BF16), v))
            new_runs.append(runs[g] + jnp.sum(lom, axis=1, keepdims=True))
        return new_accs, new_runs

    accs = [jnp.zeros((tq, hd), F32) for _ in range(hg)]
    runs = [jnp.zeros((tq, 1), F32) for _ in range(hg)]
    for j in range(nd):
        accs, runs = block(qi * nd + (nd - 1 - j), accs, runs, (nd - 1 - j) * tk)

    def run_max(rs):
        m = rs[0]
        for r in rs[1:]:
            m = jnp.maximum(m, r)
        return jnp.max(m)

    def cond(carry):
        i, top, _, _ = carry
        return jnp.logical_and(i < qi * nd, top > SB_DEAD_LOG)

    def body(carry):
        i, _, accs, runs = carry
        accs, runs = block(qi * nd - 1 - i, list(accs), list(runs), None)
        return i + 1, run_max(runs), tuple(accs), tuple(runs)

    _, _, accs, _ = lax.while_loop(cond, body, (jnp.int32(0), run_max(runs), tuple(accs), tuple(runs)))
    for g in range(hg):
        acc = accs[g]
        o = acc * lax.rsqrt(jnp.mean(acc * acc, axis=-1, keepdims=True) + RMS_EPS) * nw_ref[:, g * hd:(g + 1) * hd]
        o_ref[:, g * hd:(g + 1) * hd] = o.astype(BF16)


def _sb_call(proj, sb_norm_w, batch, seq, heads):
    t = proj.shape[0]
    tq = _pick(seq, (256, 128))
    tk = tq
    hg = 4 if heads % 4 == 0 else 2
    nq = seq // tq
    hb = heads // hg
    wd = hg * SB_HEAD_DIM
    return pl.pallas_call(
        functools.partial(_sb_kernel, tq=tq, tk=tk, hg=hg),
        out_shape=jax.ShapeDtypeStruct((t, heads * SB_HEAD_DIM), BF16),
        grid=(batch, hb, nq),
        in_specs=[
            pl.BlockSpec((tq, wd), lambda b, h, i: (b * nq + i, h)),
            pl.BlockSpec((seq, wd), lambda b, h, i: (b, hb + h)),
            pl.BlockSpec((seq, wd), lambda b, h, i: (b, 2 * hb + h)),
            pl.BlockSpec((1, wd), lambda b, h, i: (0, h)),
        ],
        out_specs=pl.BlockSpec((tq, wd), lambda b, h, i: (b * nq + i, h)),
        compiler_params=_params(("arbitrary", "arbitrary", "arbitrary")),
        name="sb_attn",
    )(proj, proj, proj, sb_norm_w.reshape(1, -1))


def _ssd_kernel(z_ref, xs_ref, b_ref, c_ref, dt_ref, dtt_ref, cw_ref, cb_ref, dtb_ref, dtbt_ref,
                alog_ref, alogt_ref, dskip_ref, nw_ref, o_ref, prev_scr, state_scr, *, heads):
    ln = xs_ref.shape[0]
    w = xs_ref.shape[1]
    gn = b_ref.shape[1]
    n = gn // SSD_GROUPS
    wg = w // SSD_GROUPS
    hpg = heads // SSD_GROUPS

    @pl.when(pl.program_id(1) == 0)
    def _():
        prev_scr[...] = jnp.zeros_like(prev_scr)
        state_scr[...] = jnp.zeros_like(state_scr)

    raw = jnp.concatenate([xs_ref[...], b_ref[...], c_ref[...]], axis=1).astype(F32)
    pr = prev_scr[...]
    ch = raw.shape[1]
    row = lax.broadcasted_iota(jnp.int32, (ln, ch), 0)
    taps = cw_ref.shape[0]
    acc = raw * cw_ref[taps - 1:taps, :] + cb_ref[...]
    for sh in range(1, taps):
        shifted = jnp.where(row < sh, pltpu.roll(pr, sh, 0), pltpu.roll(raw, sh, 0))
        acc = acc + shifted * cw_ref[taps - 1 - sh:taps - sh, :]
    prev_scr[...] = raw
    xbc = _silu(acc)
    xs = xbc[:, :w]
    bm = xbc[:, w:w + gn]
    cm = xbc[:, w + gn:]

    dtv = _softplus(dt_ref[0] + dtb_ref[...])
    dtvt = _softplus(dtt_ref[0] + dtbt_ref[...])
    a_full = -jnp.exp(alog_ref[...])
    a_t = -jnp.exp(alogt_ref[...])

    eh = lax.broadcasted_iota(jnp.int32, (heads, w), 0)
    ec = lax.broadcasted_iota(jnp.int32, (heads, w), 1)
    expand = jnp.where(ec // SSD_HEAD_DIM == eh, 1.0, 0.0).astype(BF16)
    d_hi, d_mid, d_lo = _split3(dtv)
    dt_exp = _dot(d_hi, expand) + _dot(d_mid, expand) + _dot(d_lo, expand)

    rl = lax.broadcasted_iota(jnp.int32, (ln, ln), 0)
    cl = lax.broadcasted_iota(jnp.int32, (ln, ln), 1)
    tril = cl <= rl
    incl = jnp.where(tril, 1.0, 0.0).astype(BF16)
    incl_t = jnp.where(rl <= cl, 1.0, 0.0).astype(BF16)

    a_exp = dt_exp * a_full
    a_hi, a_mid, a_lo = _split3(a_exp)
    acum = _dot(incl, a_hi) + _dot(incl, a_mid) + _dot(incl, a_lo)
    at = dtvt * a_t
    t_hi, t_mid, t_lo = _split3(at)
    acum_t = _dot(t_hi, incl_t) + _dot(t_mid, incl_t) + _dot(t_lo, incl_t)

    xdt = xs * dt_exp
    last = acum[ln - 1:ln, :]
    xdd = (xdt * jnp.exp(last - acum)).astype(BF16)
    xdtb = xdt.astype(BF16)
    ea = jnp.exp(acum)
    cdec = jnp.exp(last)

    col_head = lax.broadcasted_iota(jnp.int32, (ln, wg), 1) // SSD_HEAD_DIM
    ys = []
    for g in range(SSD_GROUPS):
        bg = bm[:, g * n:(g + 1) * n].astype(BF16)
        cg = cm[:, g * n:(g + 1) * n].astype(BF16)
        cb = _dot_nt(cg, bg)
        ms = []
        for e in range(hpg):
            h = g * hpg + e
            ac = acum[:, h * SSD_HEAD_DIM:h * SSD_HEAD_DIM + 1]
            ar = acum_t[h:h + 1, :]
            lmat = jnp.exp(jnp.where(tril, ac - ar, -jnp.inf))
            ms.append((cb * lmat).astype(BF16))
        mcat = jnp.concatenate(ms, axis=1)
        xg = xdtb[:, g * wg:(g + 1) * wg]
        xbd = jnp.concatenate([jnp.where(col_head == e, xg, jnp.zeros_like(xg)) for e in range(hpg)], axis=0)
        y_diag = _dot(mcat, xbd)
        sg = state_scr[g]
        y_off = _dot(cg, sg.astype(BF16)) * ea[:, g * wg:(g + 1) * wg]
        s_new = _dot_tn(bg, xdd[:, g * wg:(g + 1) * wg])
        state_scr[g] = sg * cdec[:, g * wg:(g + 1) * wg] + s_new
        ys.append(y_diag + y_off)
    y = jnp.concatenate(ys, axis=1) + xs * dskip_ref[...]
    y = y * _silu(z_ref[...].astype(F32))
    o = y * lax.rsqrt(jnp.mean(y * y, axis=-1, keepdims=True) + RMS_EPS) * nw_ref[...]
    o_ref[...] = o.astype(BF16)


def _ssd_call(proj, dt, conv_w, conv_b, dt_bias, a_log, d_skip, ssd_norm_w, batch, seq, sbw):
    t = proj.shape[0]
    heads = dt_bias.shape[0]
    w = heads * SSD_HEAD_DIM
    gn = SSD_GROUPS * SSD_STATE
    ch = w + 2 * gn
    ln = SSD_CHUNK
    nc = seq // ln
    dt3 = dt.reshape(batch, seq, heads)
    dtt = jnp.swapaxes(dt3, 1, 2)
    rep = lambda v: jnp.repeat(v, SSD_HEAD_DIM).reshape(1, w)
    z_blk, xs_blk = 3 * sbw // w, (3 * sbw + w) // w
    b_blk, c_blk = (3 * sbw + 2 * w) // gn, (3 * sbw + 2 * w + gn) // gn
    assert 3 * sbw % w == 0 and (3 * sbw + 2 * w) % gn == 0
    full = lambda shape: pl.BlockSpec(shape, lambda b, c: (0,) * len(shape))
    return pl.pallas_call(
        functools.partial(_ssd_kernel, heads=heads),
        out_shape=jax.ShapeDtypeStruct((t, w), BF16),
        grid=(batch, nc),
        in_specs=[
            pl.BlockSpec((ln, w), lambda b, c: (b * nc + c, z_blk)),
            pl.BlockSpec((ln, w), lambda b, c: (b * nc + c, xs_blk)),
            pl.BlockSpec((ln, gn), lambda b, c: (b * nc + c, b_blk)),
            pl.BlockSpec((ln, gn), lambda b, c: (b * nc + c, c_blk)),
            pl.BlockSpec((1, ln, heads), lambda b, c: (b, c, 0)),
            pl.BlockSpec((1, heads, ln), lambda b, c: (b, 0, c)),
            full(conv_w.shape), full((1, ch)), full((1, heads)), full((heads, 1)),
            full((1, w)), full((heads, 1)), full((1, w)), full((1, w)),
        ],
        out_specs=pl.BlockSpec((ln, w), lambda b, c: (b * nc + c, 0)),
        scratch_shapes=[pltpu.VMEM((ln, ch), F32), pltpu.VMEM((SSD_GROUPS, SSD_STATE, w // SSD_GROUPS), F32)],
        compiler_params=_params(("arbitrary", "arbitrary")),
        name="ssd",
    )(proj, proj, proj, proj, dt3, dtt, conv_w, conv_b.reshape(1, ch), dt_bias.reshape(1, heads),
      dt_bias.reshape(heads, 1), rep(a_log), a_log.reshape(heads, 1), rep(d_skip), ssd_norm_w.reshape(1, w))


def _outproj_kernel(osb_ref, ossd_ref, x_ref, mod_ref, w1_ref, w2_ref, g_ref, b_ref, x1_ref, h2_ref, *, alpha):
    m = _dot(osb_ref[...], w1_ref[...]) + _dot(ossd_ref[...], w2_ref[...])
    v = alpha * x_ref[...] + (1.0 + mod_ref[0, 2:3, :]) * m
    x1 = _ln_stats(v) * g_ref[...] + b_ref[...]
    x1_ref[...] = x1
    h2_ref[...] = _pack_halves(_ln_stats(x1) * (1.0 + mod_ref[0, 4:5, :]) + mod_ref[0, 3:4, :])


def _outproj_call(o_sb, o_ssd, x2d, mod, w_out_b, ln_g, ln_b, seq, alpha):
    t, d = x2d.shape
    sbw, ssw = o_sb.shape[1], o_ssd.shape[1]
    tm = _pick(seq, (256, 128))
    per_b = seq // tm
    return pl.pallas_call(
        functools.partial(_outproj_kernel, alpha=alpha),
        out_shape=(jax.ShapeDtypeStruct((t, d), F32), jax.ShapeDtypeStruct((t, d // 2), jnp.uint32)),
        grid=(t // tm,),
        in_specs=[
            pl.BlockSpec((tm, sbw), lambda i: (i, 0)),
            pl.BlockSpec((tm, ssw), lambda i: (i, 0)),
            pl.BlockSpec((tm, d), lambda i: (i, 0)),
            pl.BlockSpec((1, 6, d), lambda i: (i // per_b, 0, 0)),
            pl.BlockSpec((sbw, d), lambda i: (0, 0)),
            pl.BlockSpec((ssw, d), lambda i: (0, 0)),
            pl.BlockSpec((1, d), lambda i: (0, 0)),
            pl.BlockSpec((1, d), lambda i: (0, 0)),
        ],
        out_specs=(pl.BlockSpec((tm, d), lambda i: (i, 0)), pl.BlockSpec((tm, d // 2), lambda i: (i, 0))),
        compiler_params=_params(("arbitrary",)),
        name="out_proj",
    )(o_sb, o_ssd, x2d, mod, w_out_b[:sbw], w_out_b[sbw:], ln_g.reshape(1, d), ln_b.reshape(1, d))


def _router_kernel(h_ref, wr_ref, bias_ref, idx_ref, rnk_ref, gate_ref, cnt_ref, carry_scr):
    ne = wr_ref.shape[0]
    tm = h_ref.shape[0]
    per_g = ne // N_EXPERT_GROUPS
    neg = -jnp.inf

    @pl.when(pl.program_id(0) == 0)
    def _():
        carry_scr[...] = jnp.zeros_like(carry_scr)

    ha, hb = _unpack_halves(h_ref[...])
    half = ha.shape[1]
    logits = _dot_nt(wr_ref[:, :half], ha.astype(BF16)) + _dot_nt(wr_ref[:, half:], hb.astype(BF16))
    scores = jax.nn.sigmoid(logits)
    biased = scores + bias_ref[...]
    s3 = scores.reshape(N_EXPERT_GROUPS, per_g, tm)
    b3 = biased.reshape(N_EXPERT_GROUPS, per_g, tm)
    shape3 = (N_EXPERT_GROUPS, per_g, tm)
    j_io = lax.broadcasted_iota(jnp.int32, shape3, 1)
    g_io = lax.broadcasted_iota(jnp.int32, shape3, 0)
    e_io = g_io * per_g + j_io

    m1 = jnp.max(b3, axis=1, keepdims=True)
    i1 = jnp.min(jnp.where(b3 == m1, j_io, per_g), axis=1, keepdims=True)
    m2 = jnp.max(jnp.where(j_io == i1, neg, b3), axis=1, keepdims=True)
    gs = m1 + m2
    gg = lax.broadcasted_iota(jnp.int32, gs.shape, 0)
    gmask = jnp.zeros(gs.shape, F32)
    cur = gs
    for _ in range(TOPK_GROUPS):
        m = jnp.max(cur, axis=0, keepdims=True)
        gi = jnp.min(jnp.where(cur == m, gg, N_EXPERT_GROUPS), axis=0, keepdims=True)
        pick = gg == gi
        gmask = jnp.where(pick, 1.0, gmask)
        cur = jnp.where(pick, neg, cur)
    cur = jnp.where(jnp.broadcast_to(gmask, shape3) > 0.5, b3, neg)

    sel = jnp.zeros(shape3, F32)
    picks = []
    for _ in range(TOP_K):
        m = jnp.max(jnp.max(cur, axis=1, keepdims=True), axis=0, keepdims=True)
        ei = jnp.min(jnp.min(jnp.where(cur == m, e_io, ne), axis=1, keepdims=True), axis=0, keepdims=True)
        pick = e_io == ei
        sel = jnp.where(pick, 1.0, sel)
        cur = jnp.where(pick, neg, cur)
        picks.append((pick, ei))

    selw = sel * s3
    denom = jnp.sum(jnp.sum(selw, axis=1, keepdims=True), axis=0, keepdims=True)
    gates3 = selw / denom * ROUTED_SCALE

    sel2 = sel.reshape(ne, tm)
    rj = lax.broadcasted_iota(jnp.int32, (tm, tm), 0)
    cs = lax.broadcasted_iota(jnp.int32, (tm, tm), 1)
    before = jnp.where(rj < cs, 1.0, 0.0).astype(BF16)
    carry = carry_scr[...]
    rank2 = _dot(sel2.astype(BF16), before) + carry[:, 0:1]
    rank3 = rank2.reshape(shape3)
    new_carry = carry + jnp.sum(sel2, axis=1, keepdims=True)
    carry_scr[...] = new_carry
    cnt_ref[...] = new_carry.astype(jnp.int32)

    for k, (pick, ei) in enumerate(picks):
        red = lambda v: jnp.sum(jnp.sum(v, axis=1, keepdims=True), axis=0, keepdims=True).reshape(1, tm)
        idx_ref[k:k + 1, :] = ei.reshape(1, tm)
        rnk_ref[k:k + 1, :] = red(jnp.where(pick, rank3, 0.0)).astype(jnp.int32)
        gate_ref[k:k + 1, :] = red(jnp.where(pick, gates3, 0.0))


def _router_call(h2p, wr_t, bias):
    t = h2p.shape[0]
    ne, d = wr_t.shape
    tm = _pick(t, (512, 256))
    if t == tm:
        tm = tm // 2
    return pl.pallas_call(
        _router_kernel,
        out_shape=(jax.ShapeDtypeStruct((TOP_K, t), jnp.int32), jax.ShapeDtypeStruct((TOP_K, t), jnp.int32),
                   jax.ShapeDtypeStruct((TOP_K, t), F32), jax.ShapeDtypeStruct((ne, 128), jnp.int32)),
        grid=(t // tm,),
        in_specs=[
            pl.BlockSpec((tm, d // 2), lambda i: (i, 0)),
            pl.BlockSpec((ne, d), lambda i: (0, 0)),
            pl.BlockSpec((ne, 1), lambda i: (0, 0)),
        ],
        out_specs=(pl.BlockSpec((TOP_K, tm), lambda i: (0, i)), pl.BlockSpec((TOP_K, tm), lambda i: (0, i)),
                   pl.BlockSpec((TOP_K, tm), lambda i: (0, i)), pl.BlockSpec((ne, 128), lambda i: (0, 0))),
        scratch_shapes=[pltpu.VMEM((ne, 128), F32)],
        compiler_params=_params(("arbitrary",)),
        name="router",
    )(h2p, wr_t, bias.reshape(ne, 1))


def _dispatch_kernel(ps_ref, pc_ref, nu_ref, dest_ref, h_ref, meta_ref, xs_hbm, zbuf, rowbuf, idx_smem,
                     isem, ssem, zsem, tsem):
    tm = h_ref.shape[0]
    pw = h_ref.shape[1]
    ne = ps_ref.shape[0]
    tb = zbuf.shape[0]
    nblk = xs_hbm.shape[0] // tb

    @pl.when(pl.program_id(0) == 0)
    def _():
        zbuf[...] = jnp.zeros_like(zbuf)

        def tail(r, c):
            pltpu.make_async_copy(zbuf, xs_hbm.at[pl.ds(pl.multiple_of(r * tb, tb), tb)], tsem).start()
            return c

        def tail_drain(r, c):
            pltpu.make_async_copy(zbuf, xs_hbm.at[pl.ds(0, tb)], tsem).wait()
            return c

        lax.fori_loop(nu_ref[0], nblk, tail, 0)
        lax.fori_loop(nu_ref[0], nblk, tail_drain, 0)

        def fill(e, c):
            def one(j, c2):
                pltpu.make_async_copy(zbuf.at[0], xs_hbm.at[ps_ref[e] + j], zsem).start()
                return c2
            return lax.fori_loop(0, pc_ref[e], one, c)

        def drain(e, c):
            def one(j, c2):
                pltpu.make_async_copy(zbuf.at[0], xs_hbm.at[0], zsem).wait()
                return c2
            return lax.fori_loop(0, pc_ref[e], one, c)

        lax.fori_loop(0, ne, fill, 0)
        lax.fori_loop(0, ne, drain, 0)

    cp = pltpu.make_async_copy(dest_ref.at[0, 0], idx_smem, isem)
    cp.start()
    cp.wait()
    rowbuf[:, :pw] = h_ref[...]
    rowbuf[:, pw:] = meta_ref[...]

    def issue(i, c):
        for k in range(TOP_K):
            pltpu.make_async_copy(rowbuf.at[i], xs_hbm.at[idx_smem[k * tm + i]], ssem).start()
        return c

    lax.fori_loop(0, tm, issue, 0, unroll=2)
    pltpu.make_async_copy(xs_hbm.at[pl.ds(0, TOP_K * tm)], xs_hbm.at[pl.ds(0, TOP_K * tm)], ssem).wait()


def _dispatch_call(pad_start, pad_cnt, n_used, dest, h2p, meta, n_rows, tb):
    t, pw = h2p.shape
    d = pw + meta.shape[1]
    tm = _pick(t, (512, 256))
    nt = t // tm
    dest_t = dest.reshape(TOP_K, nt, tm).transpose(1, 0, 2).reshape(nt, 1, TOP_K * tm)
    grid_spec = pltpu.PrefetchScalarGridSpec(
        num_scalar_prefetch=3,
        grid=(nt,),
        in_specs=[
            pl.BlockSpec((1, 1, TOP_K * tm), lambda i, ps, pc, nu: (i, 0, 0)),
            pl.BlockSpec((tm, pw), lambda i, ps, pc, nu: (i, 0)),
            pl.BlockSpec((tm, meta.shape[1]), lambda i, ps, pc, nu: (i, 0)),
        ],
        out_specs=pl.BlockSpec(memory_space=pl.ANY),
        scratch_shapes=[pltpu.VMEM((tb, d), h2p.dtype), pltpu.VMEM((tm, d), h2p.dtype),
                        pltpu.SMEM((TOP_K * tm,), jnp.int32), pltpu.SemaphoreType.DMA,
                        pltpu.SemaphoreType.DMA, pltpu.SemaphoreType.DMA, pltpu.SemaphoreType.DMA],
    )
    return pl.pallas_call(
        _dispatch_kernel,
        out_shape=jax.ShapeDtypeStruct((n_rows, d), h2p.dtype),
        grid_spec=grid_spec,
        compiler_params=_params(("arbitrary",)),
        name="dispatch",
    )(pad_start, pad_cnt, n_used, dest_t, h2p, meta)


def _expert_kernel(be_ref, nu_ref, x_ref, wg_ref, wu_ref, wd_ref, y_hbm, wgb, wub, wdb, obuf, dstv, dst_smem,
                   isem, osem, *, tm, n_slots):
    r = pl.program_id(0)
    tb = x_ref.shape[0]
    pw = x_ref.shape[1] - META_W
    slot = r % 2
    prev = 1 - slot
    last = nu_ref[0] - 1

    def wait_rows(s):
        pltpu.make_async_copy(obuf.at[s], y_hbm.at[pl.ds(0, tb)], osem.at[s]).wait()

    def start_row(s, i, priority=0):
        pltpu.make_async_copy(obuf.at[s, i], y_hbm.at[dst_smem[s, i]], osem.at[s]).start(priority=priority)

    def index_copy(s):
        return pltpu.make_async_copy(dstv.at[0], dst_smem.at[s], isem)

    @pl.when(r == 0)
    def _():
        obuf[...] = jnp.zeros_like(obuf)
        cp = pltpu.make_async_copy(obuf.at[0], y_hbm.at[pl.ds(n_slots, tb)], osem.at[0])
        cp.start()
        cp.wait()

        def init(i, c):
            dst_smem[1, i] = n_slots + tb + i
            return c

        lax.fori_loop(0, tb, init, 0)

    @pl.when(r <= last)
    def _():
        @pl.when(r >= 1)
        def _():
            wait_rows(slot)
            index_copy(prev).wait()

        @pl.when(jnp.logical_or(r == 0, be_ref[r] != be_ref[jnp.maximum(r - 1, 0)]))
        def _():
            wgb[...] = wg_ref[0, 0].astype(BF16)
            wub[...] = wu_ref[0, 0].astype(BF16)
            wdb[...] = wd_ref[0, 0].astype(BF16)

        for i in range(tb):
            start_row(prev, i, i % 2)

        xa, xb = _unpack_halves(x_ref[:, :pw])
        xa, xb = xa.astype(BF16), xb.astype(BF16)
        g = _dot(xa, wgb[:pw, :]) + _dot(xb, wgb[pw:, :])
        u = _dot(xa, wub[:pw, :]) + _dot(xb, wub[pw:, :])
        obuf[slot] = _pack_halves(_dot((_silu(g) * u).astype(BF16), wdb[...]))

        meta = lax.bitcast_convert_type(x_ref[:, pw:], jnp.int32)
        lane = lax.broadcasted_iota(jnp.int32, meta.shape, 1)
        hit = jnp.logical_and(meta == be_ref[r], jnp.logical_and(lane >= META_IDX, lane < META_IDX + TOP_K))
        kpos = jnp.sum(jnp.where(hit, lane - META_IDX, 0), axis=1, keepdims=True)
        tok = meta[:, META_TOK:META_TOK + 1]
        spare = n_slots + slot * tb + lax.broadcasted_iota(jnp.int32, (tb, 1), 0)
        home = (jnp.right_shift(tok, tm.bit_length() - 1) * TOP_K + kpos) * tm + jnp.bitwise_and(tok, tm - 1)
        dst = jnp.where(meta[:, 0:1] == 1, home, spare)
        dst_t = jnp.transpose(jnp.broadcast_to(dst.astype(F32), (tb, META_W)))
        dstv[...] = dst_t[0:8, :].astype(jnp.int32)
        index_copy(slot).start()

        @pl.when(r == last)
        def _():
            index_copy(slot).wait()

            def issue(i, c):
                start_row(slot, i)
                return c

            lax.fori_loop(0, tb, issue, 0, unroll=8)
            wait_rows(prev)
            wait_rows(slot)


def _expert_call(block_expert, n_used, xs, w_gate, w_up, w_down, layer, tb, tm, n_tok):
    n_rows, xw = xs.shape
    pw = xw - META_W
    d, de = w_gate.shape[2:]
    nblk = n_rows // tb
    n_slots = n_tok * TOP_K
    assert tm & (tm - 1) == 0
    blk = lambda r, be, nu: jnp.minimum(r, nu[0] - 1)
    wmap = lambda r, be, nu: (layer, be[blk(r, be, nu)], 0, 0)
    grid_spec = pltpu.PrefetchScalarGridSpec(
        num_scalar_prefetch=2,
        grid=(nblk,),
        in_specs=[
            pl.BlockSpec((tb, xw), lambda r, be, nu: (blk(r, be, nu), 0)),
            pl.BlockSpec((1, 1, d, de), wmap),
            pl.BlockSpec((1, 1, d, de), wmap),
            pl.BlockSpec((1, 1, de, d), wmap),
        ],
        out_specs=pl.BlockSpec(memory_space=pl.ANY),
        scratch_shapes=[pltpu.VMEM((d, de), BF16), pltpu.VMEM((d, de), BF16), pltpu.VMEM((de, d), BF16),
                        pltpu.VMEM((2, tb, pw), jnp.uint32), pltpu.VMEM((8, tb), jnp.int32),
                        pltpu.SMEM((2, tb), jnp.int32), pltpu.SemaphoreType.DMA, pltpu.SemaphoreType.DMA((2,))],
    )
    return pl.pallas_call(
        functools.partial(_expert_kernel, tm=tm, n_slots=n_slots),
        out_shape=jax.ShapeDtypeStruct((n_slots + 2 * tb, pw), jnp.uint32),
        grid_spec=grid_spec,
        compiler_params=_params(("arbitrary",)),
        name="experts",
    )(block_expert, n_used, xs, w_gate, w_up, w_down)


def _combine_kernel(y_ref, gate_ref, h_ref, x_ref, mod_ref, sg_ref, su_ref, sd_ref, g_ref, b_ref, out_ref, *, alpha):
    tm = h_ref.shape[0]
    ha, hb = _unpack_halves(h_ref[...])
    ha, hb = ha.astype(BF16), hb.astype(BF16)
    half = ha.shape[1]
    sg = _dot(ha, sg_ref[:half, :]) + _dot(hb, sg_ref[half:, :])
    su = _dot(ha, su_ref[:half, :]) + _dot(hb, su_ref[half:, :])
    f = _dot((_silu(sg) * su).astype(BF16), sd_ref[...])
    fa, fb = f[:, :half], f[:, half:]
    gate = gate_ref[...]
    for k in range(TOP_K):
        oa, ob = _unpack_halves(y_ref[pl.ds(k * tm, tm), :])
        fa = fa + oa * gate[:, k:k + 1]
        fb = fb + ob * gate[:, k:k + 1]
    f = jnp.concatenate([fa, fb], axis=1)
    v = alpha * x_ref[...] + (1.0 + mod_ref[0, 5:6, :]) * f
    out_ref[...] = _ln_stats(v) * g_ref[...] + b_ref[...]


def _combine_call(y8, gate_t, h2p, x1, mod, sg, su, sd, ln_g, ln_b, seq, alpha, tm):
    t, d = x1.shape
    ds = sg.shape[1]
    per_b = seq // tm
    return pl.pallas_call(
        functools.partial(_combine_kernel, alpha=alpha),
        out_shape=jax.ShapeDtypeStruct((t, d), F32),
        grid=(t // tm,),
        in_specs=[
            pl.BlockSpec((TOP_K * tm, d // 2), lambda i: (i, 0)),
            pl.BlockSpec((tm, TOP_K), lambda i: (i, 0)),
            pl.BlockSpec((tm, d // 2), lambda i: (i, 0)),
            pl.BlockSpec((tm, d), lambda i: (i, 0)),
            pl.BlockSpec((1, 6, d), lambda i: (i // per_b, 0, 0)),
            pl.BlockSpec((d, ds), lambda i: (0, 0)),
            pl.BlockSpec((d, ds), lambda i: (0, 0)),
            pl.BlockSpec((ds, d), lambda i: (0, 0)),
            pl.BlockSpec((1, d), lambda i: (0, 0)),
            pl.BlockSpec((1, d), lambda i: (0, 0)),
        ],
        out_specs=pl.BlockSpec((tm, d), lambda i: (i, 0)),
        compiler_params=_params(("arbitrary",)),
        name="combine",
    )(y8, gate_t, h2p, x1, mod, sg, su, sd, ln_g.reshape(1, d), ln_b.reshape(1, d))


def _moe(h2, x1, mod, wr_t, router_bias, w_gate, w_up, w_down, layer, sg, su, sd, ln_g, ln_b, seq, alpha, tb):
    t, d = x1.shape
    ne = w_gate.shape[1]
    idx, rnk, gate, cnt = _router_call(h2, wr_t, router_bias)
    counts = cnt[:, 0]
    padded = (counts + tb - 1) // tb * tb
    pend = jnp.cumsum(padded)
    pstart = pend - padded
    eids = jnp.arange(ne, dtype=jnp.int32)
    dest = rnk + jnp.sum(jnp.where(idx[..., None] == eids, pstart.astype(jnp.int32), 0), axis=-1)
    nblk = (t * TOP_K) // tb + ne
    blk_row = jnp.arange(nblk, dtype=jnp.int32) * tb
    block_expert = jnp.minimum(jnp.sum((pend[None, :] <= blk_row[:, None]).astype(jnp.int32), axis=1), ne - 1)
    n_used = (pend[-1] // tb).astype(jnp.int32).reshape(1)
    zeros = lambda n: jnp.zeros((t, n), jnp.int32)
    meta = jnp.concatenate([jnp.ones((t, 1), jnp.int32), jnp.arange(t, dtype=jnp.int32)[:, None],
                            zeros(META_IDX - 2), idx.T, zeros(META_W - META_IDX - TOP_K)], axis=1)
    xs = _dispatch_call((pstart + counts).astype(jnp.int32), (padded - counts).astype(jnp.int32), n_used, dest, h2,
                        lax.bitcast_convert_type(meta, jnp.uint32), nblk * tb, tb)
    tm = _pick(seq, (256, 128))
    y8 = _expert_call(block_expert, n_used, xs, w_gate, w_up, w_down, layer, tb, tm, t)
    return _combine_call(y8, gate.T, h2, x1, mod, sg, su, sd, ln_g, ln_b, seq, alpha, tm)


def kernel(x, c, w_ada, b_ada, w_in, conv_w, conv_b, dt_bias, a_log, d_skip, sb_norm_w, ssd_norm_w, w_out, ln1_g,
           ln1_b, w_router, router_bias, w_gate, w_up, w_down, ws_gate, ws_up, ws_down, ln2_g, ln2_b):
    batch, seq, d = x.shape
    depth = w_ada.shape[0]
    sbw = sb_norm_w.shape[1]
    sb_heads = sbw // SB_HEAD_DIM
    ssd_heads = dt_bias.shape[1]
    n_main = w_in.shape[2] - ssd_heads
    alpha = (2 * depth) ** 0.25
    tb = 512 if (batch * seq * TOP_K) // w_gate.shape[1] >= 2048 else 128

    mod_all = _ada_call(c, w_ada, b_ada).reshape(depth, batch, 6, d)
    x2d = x.reshape(batch * seq, d)
    for l in range(depth):
        mod = mod_all[l]
        w_main = w_in[l, :, :n_main].astype(BF16)
        w_dt = jnp.pad(w_in[l, :, n_main:], ((0, 0), (0, 128 - ssd_heads))).astype(BF16)
        proj, dtp = _inproj_call(x2d, mod, w_main, w_dt, seq)
        o_sb = _sb_call(proj, sb_norm_w[l], batch, seq, sb_heads)
        o_ssd = _ssd_call(proj, dtp[:, :ssd_heads], conv_w[l], conv_b[l], dt_bias[l], a_log[l], d_skip[l],
                          ssd_norm_w[l], batch, seq, sbw)
        x1, h2 = _outproj_call(o_sb, o_ssd, x2d, mod, w_out[l].astype(BF16), ln1_g[l], ln1_b[l], seq, alpha)
        x2d = _moe(h2, x1, mod, w_router[l].T.astype(BF16), router_bias[l], w_gate, w_up, w_down, l,
                   ws_gate[l].astype(BF16), ws_up[l].astype(BF16), ws_down[l].astype(BF16), ln2_g[l], ln2_b[l],
                   seq, alpha, tb)
    return x2d.reshape(batch, seq, d)
```
